```python
import math
import jax, jax.numpy as jnp
from jax import lax
import numpy as np

D_MODEL = 2048
BATCH = 8
SEQ = 4096
DEPTH = 4
DEC_BATCH = 4
DEC_SEQ = 4096
PAST_LEN = 128

N_MIXERS = 2
MIX_WIDTH = 3 * D_MODEL // 4
MEM_WIDTH = D_MODEL // 4
HEAD_DIM = 128
N_Q_HEADS = MIX_WIDTH // HEAD_DIM
N_KV_HEADS = 4
GQA_GROUP = N_Q_HEADS // N_KV_HEADS
KV_WIDTH = N_KV_HEADS * HEAD_DIM
WINDOW = 128
BLOCK = 128
ROPE_THETA = 10000.0
N_MEM = 256
N_MEM_HEADS = 4
MEM_HEAD_DIM = MEM_WIDTH // N_MEM_HEADS
SSM_GROUP = 16
SSM_GROUPS = MIX_WIDTH // SSM_GROUP
SSM_STATE = 64
SSM_CHUNK = 128
D_FF = 4 * D_MODEL
ALPHA = (2 * DEPTH) ** 0.25
BETA = (8 * DEPTH) ** -0.25
N_SSM_LAYERS = (DEPTH + 1) // 2
N_ATTN_LAYERS = DEPTH // 2
LN_EPS = 1e-5

kernel_name = 'hybrid_s5_window_gqa_memory_encoder'


def layer_norm(x, g, b):
    xf = x.astype(jnp.float32)
    mu = xf.mean(-1, keepdims=True)
    var = jnp.mean(jnp.square(xf - mu), -1, keepdims=True)
    return ((xf - mu) * lax.rsqrt(var + LN_EPS) * g.astype(jnp.float32) + b.astype(jnp.float32)).astype(x.dtype)


def _ssm_combine(left, right):
    a_l, b_l = left
    a_r, b_r = right
    return a_r * a_l, a_r * b_l + b_r


def s5_scan(u, lam_re, lam_im, log_dt, b_re, b_im, c_re, c_im):
    bsz, L = u.shape[0], u.shape[1]
    f32 = jnp.float32
    lam = lax.complex(lam_re.astype(f32), lam_im.astype(f32))
    dt = jnp.exp(log_dt.astype(f32))[:, None]
    lam_bar = jnp.exp(lam * dt)
    b_bar = ((lam_bar - 1.0) / lam)[..., None] * lax.complex(b_re.astype(f32), b_im.astype(f32))
    cmat = lax.complex(c_re.astype(f32), c_im.astype(f32))
    n_chunks = L // SSM_CHUNK
    uc = u.astype(f32).reshape(bsz, n_chunks, SSM_CHUNK, SSM_GROUPS, SSM_GROUP).swapaxes(0, 1)

    def step(carry, u_blk):
        bu = jnp.einsum('gpc,btgc->btgp', b_bar, u_blk)
        a = jnp.broadcast_to(lam_bar, bu.shape)
        a_cum, h = lax.associative_scan(_ssm_combine, (a, bu), axis=1)
        h = h + a_cum * carry[:, None]
        y = jnp.einsum('gcp,btgp->btgc', cmat, h).real
        return h[:, -1], y

    carry0 = jnp.zeros((bsz, SSM_GROUPS, SSM_STATE), jnp.complex64)
    _, ys = lax.scan(step, carry0, uc)
    return ys.swapaxes(0, 1).reshape(bsz, L, SSM_GROUPS, SSM_GROUP)


def s5_mixer(u, lam_re, lam_im, log_dt, b_re, b_im, c_re, c_im, d_skip, w_glu):
    bsz, L, _ = u.shape
    ug = u.reshape(bsz, L, SSM_GROUPS, SSM_GROUP)
    y_f = s5_scan(ug, lam_re[0], lam_im[0], log_dt[0], b_re[0], b_im[0], c_re[0], c_im[0])
    y_b = jnp.flip(s5_scan(jnp.flip(ug, 1), lam_re[1], lam_im[1], log_dt[1],
                           b_re[1], b_im[1], c_re[1], c_im[1]), 1)
    y = (y_f + y_b).reshape(bsz, L, MIX_WIDTH) + d_skip.astype(jnp.float32) * u.astype(jnp.float32)
    z = jax.nn.gelu(y).astype(u.dtype)
    a, g = jnp.split(z @ w_glu, 2, axis=-1)
    return a * jax.nn.sigmoid(g)


def rope_tables(L):
    inv_freq = ROPE_THETA ** (-jnp.arange(0, HEAD_DIM, 2, dtype=jnp.float32) / HEAD_DIM)
    ang = jnp.arange(L, dtype=jnp.float32)[:, None] * inv_freq[None, :]
    return jnp.cos(ang), jnp.sin(ang)


def apply_rope(x, cos, sin):
    x1, x2 = jnp.split(x.astype(jnp.float32), 2, axis=-1)
    c = cos[None, :, None, :]
    s = sin[None, :, None, :]
    return jnp.concatenate([x1 * c - x2 * s, x2 * c + x1 * s], axis=-1).astype(x.dtype)


def windowed_gqa(q, k, v, sink):
    bsz, L = q.shape[0], q.shape[1]
    nb = L // BLOCK
    qb = q.reshape(bsz, nb, BLOCK, N_KV_HEADS, GQA_GROUP, HEAD_DIM)

    def band(t):
        tb = t.reshape(bsz, nb, BLOCK, N_KV_HEADS, HEAD_DIM)
        tp = jnp.pad(tb, ((0, 0), (1, 1), (0, 0), (0, 0), (0, 0)))
        return jnp.concatenate([tp[:, :-2], tp[:, 1:-1], tp[:, 2:]], axis=2)

    kb, vb = band(k), band(v)
    s = jnp.einsum('bnqhgd,bnkhd->bnhgqk', qb, kb, preferred_element_type=jnp.float32) * (HEAD_DIM ** -0.5)
    qpos = jnp.arange(L).reshape(nb, BLOCK)
    kpos = (jnp.arange(nb)[:, None] - 1) * BLOCK + jnp.arange(3 * BLOCK)[None, :]
    rel = kpos[:, None, :] - qpos[:, :, None]
    valid = (jnp.abs(rel) <= WINDOW) & (kpos[:, None, :] >= 0) & (kpos[:, None, :] < L)
    s = jnp.where(valid[None, :, None, None], s, -1e30)
    sk = sink.astype(jnp.float32).reshape(N_KV_HEADS, GQA_GROUP)[None, None, :, :, None, None]
    m = jnp.maximum(s.max(-1, keepdims=True), sk)
    p = jnp.exp(s - m)
    p = p / (p.sum(-1, keepdims=True) + jnp.exp(sk - m))
    o = jnp.einsum('bnhgqk,bnkhd->bnqhgd', p.astype(v.dtype), vb)
    return o.reshape(bsz, L, N_Q_HEADS * HEAD_DIM)


def memory_attention(qm, mem, w_mem_kv):
    bsz, L = qm.shape[0], qm.shape[1]
    q = qm.reshape(bsz, L, N_MEM_HEADS, MEM_HEAD_DIM)
    k, v = jnp.split(mem @ w_mem_kv, 2, axis=-1)
    k = k.reshape(bsz, N_MEM, N_MEM_HEADS, MEM_HEAD_DIM)
    v = v.reshape(bsz, N_MEM, N_MEM_HEADS, MEM_HEAD_DIM)
    s = jnp.einsum('bqhd,bkhd->bhqk', q, k, preferred_element_type=jnp.float32) * (MEM_HEAD_DIM ** -0.5)
    p = jax.nn.softmax(s, axis=-1)
    o = jnp.einsum('bhqk,bkhd->bqhd', p.astype(v.dtype), v)
    return o.reshape(bsz, L, MEM_WIDTH)


def trunk(x, mem, ssm_w_in, ssm_lam_re, ssm_lam_im, ssm_log_dt, ssm_b_re, ssm_b_im,
          ssm_c_re, ssm_c_im, ssm_d, ssm_w_glu, attn_w_in, attn_sink, w_mem_kv, w_out,
          ln1_g, ln1_b, w_ff1, w_ff2, ln2_g, ln2_b):
    L = x.shape[1]
    cos, sin = rope_tables(L)
    bsz = x.shape[0]
    for i in range(DEPTH):
        j = i // N_MIXERS
        if i % N_MIXERS == 0:
            proj = x @ ssm_w_in[j]
            u, qm = proj[..., :MIX_WIDTH], proj[..., MIX_WIDTH:]
            y_mix = s5_mixer(u, ssm_lam_re[j], ssm_lam_im[j], ssm_log_dt[j], ssm_b_re[j], ssm_b_im[j],
                             ssm_c_re[j], ssm_c_im[j], ssm_d[j], ssm_w_glu[j])
        else:
            proj = x @ attn_w_in[j]
            q = proj[..., :MIX_WIDTH].reshape(bsz, L, N_Q_HEADS, HEAD_DIM)
            k = proj[..., MIX_WIDTH:MIX_WIDTH + KV_WIDTH].reshape(bsz, L, N_KV_HEADS, HEAD_DIM)
            v = proj[..., MIX_WIDTH + KV_WIDTH:MIX_WIDTH + 2 * KV_WIDTH].reshape(bsz, L, N_KV_HEADS, HEAD_DIM)
            qm = proj[..., MIX_WIDTH + 2 * KV_WIDTH:]
            y_mix = windowed_gqa(apply_rope(q, cos, sin), apply_rope(k, cos, sin), v, attn_sink[j])
        y_mem = memory_attention(qm, mem, w_mem_kv[i])
        mix = jnp.concatenate([y_mix.astype(x.dtype), y_mem.astype(x.dtype)], axis=-1) @ w_out[i]
        x = layer_norm(ALPHA * x + mix, ln1_g[i], ln1_b[i])
        h = jnp.square(jax.nn.relu(x @ w_ff1[i]))
        x = layer_norm(ALPHA * x + h @ w_ff2[i], ln2_g[i], ln2_b[i])
    return x


def setup_inputs(seed: int = 0) -> dict:
    key = jax.random.key(seed)
    ks = iter(jax.random.split(key, 32))
    f32 = jnp.float32

    def nrm(shape, scale):
        return jax.random.normal(next(ks), shape, f32) * scale

    nS, nA, G, P, C = N_SSM_LAYERS, N_ATTN_LAYERS, SSM_GROUPS, SSM_STATE, SSM_GROUP
    inp = {}
    inp['x_prompt'] = nrm((BATCH, SEQ, D_MODEL), 1.0)
    inp['x_sample'] = nrm((DEC_BATCH, DEC_SEQ, D_MODEL), 1.0)
    inp['mem_prompt'] = nrm((BATCH, N_MEM, D_MODEL), 1.0)
    inp['mem_sample'] = nrm((DEC_BATCH, N_MEM, D_MODEL), 1.0)
    inp['ssm_w_in'] = nrm((nS, D_MODEL, MIX_WIDTH + MEM_WIDTH), D_MODEL ** -0.5)
    inp['ssm_lam_re'] = -0.5 + nrm((nS, 2, G, P), 0.01)
    inp['ssm_lam_im'] = jnp.broadcast_to(jnp.pi * jnp.arange(P, dtype=f32), (nS, 2, G, P)) + nrm((nS, 2, G, P), 0.01)
    inp['ssm_log_dt'] = jax.random.uniform(next(ks), (nS, 2, G), f32, math.log(1e-3), math.log(1e-1))
    inp['ssm_b_re'] = nrm((nS, 2, G, P, C), (2 * C) ** -0.5)
    inp['ssm_b_im'] = nrm((nS, 2, G, P, C), (2 * C) ** -0.5)
    inp['ssm_c_re'] = nrm((nS, 2, G, C, P), (2 * P) ** -0.5)
    inp['ssm_c_im'] = nrm((nS, 2, G, C, P), (2 * P) ** -0.5)
    inp['ssm_d'] = nrm((nS, MIX_WIDTH), 1.0)
    inp['ssm_w_glu'] = nrm((nS, MIX_WIDTH, 2 * MIX_WIDTH), MIX_WIDTH ** -0.5)
    inp['attn_w_in'] = nrm((nA, D_MODEL, MIX_WIDTH + 2 * KV_WIDTH + MEM_WIDTH), D_MODEL ** -0.5)
    inp['attn_sink'] = nrm((nA, N_Q_HEADS), 0.5)
    inp['w_mem_kv'] = nrm((DEPTH, D_MODEL, 2 * MEM_WIDTH), D_MODEL ** -0.5)
    inp['w_out'] = nrm((DEPTH, D_MODEL, D_MODEL), BETA * D_MODEL ** -0.5)
    inp['ln1_g'] = 1.0 + nrm((DEPTH, D_MODEL), 0.02)
    inp['ln1_b'] = nrm((DEPTH, D_MODEL), 0.02)
    inp['w_ff1'] = nrm((DEPTH, D_MODEL, D_FF), D_MODEL ** -0.5)
    inp['w_ff2'] = nrm((DEPTH, D_FF, D_MODEL), BETA * D_FF ** -0.5)
    inp['ln2_g'] = 1.0 + nrm((DEPTH, D_MODEL), 0.02)
    inp['ln2_b'] = nrm((DEPTH, D_MODEL), 0.02)
    return inp


def reference(x_prompt, x_sample, mem_prompt, mem_sample, ssm_w_in, ssm_lam_re, ssm_lam_im, ssm_log_dt,
              ssm_b_re, ssm_b_im, ssm_c_re, ssm_c_im, ssm_d, ssm_w_glu, attn_w_in, attn_sink, w_mem_kv,
              w_out, ln1_g, ln1_b, w_ff1, w_ff2, ln2_g, ln2_b):
    y_prompt = trunk(x_prompt, mem_prompt, ssm_w_in, ssm_lam_re, ssm_lam_im, ssm_log_dt, ssm_b_re, ssm_b_im,
                     ssm_c_re, ssm_c_im, ssm_d, ssm_w_glu, attn_w_in, attn_sink, w_mem_kv, w_out,
                     ln1_g, ln1_b, w_ff1, w_ff2, ln2_g, ln2_b)
    y_sample = trunk(x_sample, mem_sample, ssm_w_in, ssm_lam_re, ssm_lam_im, ssm_log_dt, ssm_b_re, ssm_b_im,
                     ssm_c_re, ssm_c_im, ssm_d, ssm_w_glu, attn_w_in, attn_sink, w_mem_kv, w_out,
                     ln1_g, ln1_b, w_ff1, w_ff2, ln2_g, ln2_b)
    return (y_prompt, y_sample)
```

```python
import functools
import math

import jax
import jax.numpy as jnp
from jax import lax
from jax.experimental import pallas as pl
from jax.experimental.pallas import tpu as pltpu

F32 = jnp.float32
BF16 = jnp.bfloat16

HEAD_DIM = 128
N_KV_HEADS = 4
GQA_GROUP = 3
WINDOW = 128
N_MEM_HEADS = 4
SSM_GROUP = 16
SSM_STATE = 64
SSM_CHUNK = 128
ROPE_THETA = 10000.0
LN_EPS = 1e-5
NEG_INF = -1e30
SQ = 2 * SSM_STATE * 2
GW = SSM_GROUP * SSM_CHUNK
V7X_VMEM_LIMIT_MB = 56


def _cparams(sem, vmem_mb=V7X_VMEM_LIMIT_MB):
    return pltpu.CompilerParams(dimension_semantics=sem, vmem_limit_bytes=vmem_mb * 1024 * 1024)


def _tile(n, target):
    if n <= target:
        return n
    for t in range(target, 7, -1):
        if n % t == 0 and t % 8 == 0:
            return t
    return n


def _layer_norm(y, g, b):
    mu = jnp.mean(y, axis=-1, keepdims=True)
    d = y - mu
    var = jnp.mean(d * d, axis=-1, keepdims=True)
    return d * lax.rsqrt(var + LN_EPS) * g + b


def _rowmm_kernel(x_ref, w_ref, o_ref, *, n_chunk):
    xb = x_ref[...].astype(BF16)
    n = w_ref.shape[1]
    for n0 in range(0, n, n_chunk):
        o_ref[:, n0:n0 + n_chunk] = jnp.dot(
            xb, w_ref[:, n0:n0 + n_chunk], preferred_element_type=F32).astype(o_ref.dtype)


def _rowmm(x, w, tm_target=512):
    m, k = x.shape
    n = w.shape[1]
    tm = _tile(m, tm_target)
    n_chunk = 512 if n % 512 == 0 else n
    return pl.pallas_call(
        functools.partial(_rowmm_kernel, n_chunk=n_chunk),
        grid=(m // tm,),
        in_specs=[pl.BlockSpec((tm, k), lambda i: (i, 0)),
                  pl.BlockSpec((k, n), lambda i: (0, 0))],
        out_specs=pl.BlockSpec((tm, n), lambda i: (i, 0)),
        out_shape=jax.ShapeDtypeStruct((m, n), BF16),
        compiler_params=_cparams(("parallel",)),
    )(x, w)


def _glu_kernel(z_ref, wa_ref, wg_ref, o_ref, *, n_chunk):
    z = z_ref[...]
    n = wa_ref.shape[1]
    for n0 in range(0, n, n_chunk):
        a = jnp.dot(z, wa_ref[:, n0:n0 + n_chunk], preferred_element_type=F32)
        g = jnp.dot(z, wg_ref[:, n0:n0 + n_chunk], preferred_element_type=F32)
        o_ref[:, n0:n0 + n_chunk] = (a * jax.nn.sigmoid(g)).astype(o_ref.dtype)


def _glu(z, w_glu, tm_target=512):
    m, k = z.shape
    n = w_glu.shape[1] // 2
    tm = _tile(m, tm_target)
    n_chunk = 512 if n % 512 == 0 else n
    return pl.pallas_call(
        functools.partial(_glu_kernel, n_chunk=n_chunk),
        grid=(m // tm,),
        in_specs=[pl.BlockSpec((tm, k), lambda i: (i, 0)),
                  pl.BlockSpec((k, n), lambda i: (0, 0)),
                  pl.BlockSpec((k, n), lambda i: (0, 1))],
        out_specs=pl.BlockSpec((tm, n), lambda i: (i, 0)),
        out_shape=jax.ShapeDtypeStruct((m, n), BF16),
        compiler_params=_cparams(("parallel",)),
    )(z, w_glu, w_glu)


def _oproj_kernel(ymix_ref, ymem_ref, wa_ref, wb_ref, x_ref, g_ref, b_ref, o_ref, *, alpha):
    acc = jnp.dot(ymix_ref[...], wa_ref[...], preferred_element_type=F32)
    acc = acc + jnp.dot(ymem_ref[...], wb_ref[...], preferred_element_type=F32)
    y = alpha * x_ref[...] + acc
    o_ref[...] = _layer_norm(y, g_ref[...], b_ref[...])


def _oproj(ymix, ymem, w_out, x, g, b, alpha, tm_target=512):
    m, d = x.shape
    kmix = ymix.shape[1]
    kmem = ymem.shape[1]
    tm = _tile(m, tm_target)
    return pl.pallas_call(
        functools.partial(_oproj_kernel, alpha=alpha),
        grid=(m // tm,),
        in_specs=[pl.BlockSpec((tm, kmix), lambda i: (i, 0)),
                  pl.BlockSpec((tm, kmem), lambda i: (i, 0)),
                  pl.BlockSpec((kmix, d), lambda i: (0, 0)),
                  pl.BlockSpec((kmem, d), lambda i: (kmix // kmem, 0)),
                  pl.BlockSpec((tm, d), lambda i: (i, 0)),
                  pl.BlockSpec((1, d), lambda i: (0, 0)),
                  pl.BlockSpec((1, d), lambda i: (0, 0))],
        out_specs=pl.BlockSpec((tm, d), lambda i: (i, 0)),
        out_shape=jax.ShapeDtypeStruct((m, d), F32),
        compiler_params=_cparams(("parallel",)),
    )(ymix, ymem, w_out, w_out, x, g.reshape(1, d), b.reshape(1, d))


def _ffn_kernel(x_ref, w1_ref, w2_ref, g_ref, b_ref, o_ref, xb_ref, acc_ref, *, alpha):
    f = pl.program_id(1)

    @pl.when(f == 0)
    def _():
        xb_ref[...] = x_ref[...].astype(BF16)

    h = jnp.dot(xb_ref[...], w1_ref[...], preferred_element_type=F32)
    h = jnp.square(jnp.maximum(h, 0.0)).astype(BF16)
    part = jnp.dot(h, w2_ref[...], preferred_element_type=F32)

    @pl.when(f == 0)
    def _():
        acc_ref[...] = part

    @pl.when(f > 0)
    def _():
        acc_ref[...] += part

    @pl.when(f == pl.num_programs(1) - 1)
    def _():
        y = alpha * x_ref[...] + acc_ref[...]
        o_ref[...] = _layer_norm(y, g_ref[...], b_ref[...])


def _ffn(x, w1, w2, g, b, alpha, tm_target=512, tf_target=1024):
    m, d = x.shape
    dff = w1.shape[1]
    tm = _tile(m, tm_target)
    tf = _tile(dff, tf_target)
    return pl.pallas_call(
        functools.partial(_ffn_kernel, alpha=alpha),
        grid=(m // tm, dff // tf),
        in_specs=[pl.BlockSpec((tm, d), lambda i, f: (i, 0)),
                  pl.BlockSpec((d, tf), lambda i, f: (0, f)),
                  pl.BlockSpec((tf, d), lambda i, f: (f, 0)),
                  pl.BlockSpec((1, d), lambda i, f: (0, 0)),
                  pl.BlockSpec((1, d), lambda i, f: (0, 0))],
        out_specs=pl.BlockSpec((tm, d), lambda i, f: (i, 0)),
        out_shape=jax.ShapeDtypeStruct((m, d), F32),
        scratch_shapes=[pltpu.VMEM((tm, d), BF16), pltpu.VMEM((tm, d), F32)],
        compiler_params=_cparams(("parallel", "arbitrary")),
    )(x, w1, w2, g.reshape(1, d), b.reshape(1, d))


def _memattn_kernel(q_ref, k_ref, v_ref, o_ref):
    scale = HEAD_DIM ** -0.5
    for h in range(N_MEM_HEADS):
        sl = slice(h * HEAD_DIM, (h + 1) * HEAD_DIM)
        s = lax.dot_general(q_ref[0, :, sl], k_ref[0, :, sl], (((1,), (1,)), ((), ())),
                            preferred_element_type=F32) * scale
        m = jnp.max(s, axis=-1, keepdims=True)
        p = jnp.exp(s - m)
        l = jnp.sum(p, axis=-1, keepdims=True)
        o = jnp.dot(p.astype(BF16), v_ref[0, :, sl], preferred_element_type=F32)
        o_ref[0, :, sl] = (o / l).astype(o_ref.dtype)


def _memattn(proj, q_col_block, kv, tq_target=1024):
    bsz, seq, _ = proj.shape
    n_mem = kv.shape[1]
    w = N_MEM_HEADS * HEAD_DIM
    tq = _tile(seq, tq_target)
    return pl.pallas_call(
        _memattn_kernel,
        grid=(bsz, seq // tq),
        in_specs=[pl.BlockSpec((1, tq, w), lambda b, i: (b, i, q_col_block)),
                  pl.BlockSpec((1, n_mem, w), lambda b, i: (b, 0, 0)),
                  pl.BlockSpec((1, n_mem, w), lambda b, i: (b, 0, 1))],
        out_specs=pl.BlockSpec((1, tq, w), lambda b, i: (b, i, 0)),
        out_shape=jax.ShapeDtypeStruct((bsz, seq, w), BF16),
        compiler_params=_cparams(("parallel", "parallel")),
    )(proj, kv, kv)


def _rope(x, cos2, sin2):
    return x * cos2 + pltpu.roll(x, HEAD_DIM // 2, axis=1) * sin2


def _wattn_kernel(sink_ref, q_ref, kp_ref, kc_ref, kn_ref, vp_ref, vc_ref, vn_ref, cos_ref, sin_ref,
                  o_ref, qr_ref, kr_ref, vr_ref, *, tq, seq):
    i = pl.program_id(1)
    base = i * tq
    blk = WINDOW
    nqb = tq // blk
    scale = HEAD_DIM ** -0.5
    n_q_heads = N_KV_HEADS * GQA_GROUP

    cos_c = cos_ref[pl.ds(pl.multiple_of(base, blk), tq), :]
    sin_c = sin_ref[pl.ds(pl.multiple_of(base, blk), tq), :]
    p_start = pl.multiple_of(jnp.maximum(base - blk, 0), blk)
    n_start = pl.multiple_of(jnp.minimum(base + tq, seq - blk), blk)
    cos_p = cos_ref[pl.ds(p_start, blk), :]
    sin_p = sin_ref[pl.ds(p_start, blk), :]
    cos_n = cos_ref[pl.ds(n_start, blk), :]
    sin_n = sin_ref[pl.ds(n_start, blk), :]

    for h in range(n_q_heads):
        sl = slice(h * HEAD_DIM, (h + 1) * HEAD_DIM)
        qr_ref[:, sl] = (_rope(q_ref[0, :, sl].astype(F32), cos_c, sin_c) * scale).astype(BF16)
    for h in range(N_KV_HEADS):
        sl = slice(h * HEAD_DIM, (h + 1) * HEAD_DIM)
        kr_ref[0:blk, sl] = _rope(kp_ref[0, :, sl].astype(F32), cos_p, sin_p).astype(BF16)
        kr_ref[blk:blk + tq, sl] = _rope(kc_ref[0, :, sl].astype(F32), cos_c, sin_c).astype(BF16)
        kr_ref[blk + tq:, sl] = _rope(kn_ref[0, :, sl].astype(F32), cos_n, sin_n).astype(BF16)
    vr_ref[0:blk, :] = vp_ref[0]
    vr_ref[blk:blk + tq, :] = vc_ref[0]
    vr_ref[blk + tq:, :] = vn_ref[0]

    r = lax.broadcasted_iota(jnp.int32, (blk, 3 * blk), 0)
    c = lax.broadcasted_iota(jnp.int32, (blk, 3 * blk), 1)
    rel = c - blk - r
    band = jnp.abs(rel) <= WINDOW

    for jb in range(nqb):
        kpos = base + (jb - 1) * blk + c
        valid = band & (kpos >= 0) & (kpos < seq)
        bias = jnp.where(valid, 0.0, NEG_INF).astype(F32)
        for hk in range(N_KV_HEADS):
            ksl = slice(hk * HEAD_DIM, (hk + 1) * HEAD_DIM)
            kw = kr_ref[jb * blk:(jb + 3) * blk, ksl]
            vw = vr_ref[jb * blk:(jb + 3) * blk, ksl]
            qs = jnp.concatenate(
                [qr_ref[jb * blk:(jb + 1) * blk, (hk * GQA_GROUP + g) * HEAD_DIM:(hk * GQA_GROUP + g + 1) * HEAD_DIM]
                 for g in range(GQA_GROUP)], axis=0)
            s = lax.dot_general(qs, kw, (((1,), (1,)), ((), ())), preferred_element_type=F32)
            ps = []
            inv = []
            for g in range(GQA_GROUP):
                sk = sink_ref[hk * GQA_GROUP + g]
                sg = s[g * blk:(g + 1) * blk, :] + bias
                m = jnp.maximum(jnp.max(sg, axis=-1, keepdims=True), sk)
                p = jnp.exp(sg - m)
                den = jnp.sum(p, axis=-1, keepdims=True) + jnp.exp(sk - m)
                ps.append(p.astype(BF16))
                inv.append(1.0 / den)
            o = jnp.dot(jnp.concatenate(ps, axis=0), vw, preferred_element_type=F32)
            for g in range(GQA_GROUP):
                hq = hk * GQA_GROUP + g
                o_ref[0, jb * blk:(jb + 1) * blk, hq * HEAD_DIM:(hq + 1) * HEAD_DIM] = (
                    o[g * blk:(g + 1) * blk, :] * inv[g]).astype(o_ref.dtype)


def _wattn(proj, sink, cos2, sin2, tq_target=512):
    bsz, seq, _ = proj.shape
    blk = WINDOW
    qw = N_KV_HEADS * GQA_GROUP * HEAD_DIM
    kvw = N_KV_HEADS * HEAD_DIM
    tq = _tile(seq, tq_target)
    r = tq // blk
    nb = seq // blk
    kcol = qw // kvw
    vcol = kcol + 1
    in_specs = [
        pl.BlockSpec(memory_space=pltpu.SMEM),
        pl.BlockSpec((1, tq, qw), lambda b, i: (b, i, 0)),
        pl.BlockSpec((1, blk, kvw), lambda b, i: (b, jnp.maximum(i * r - 1, 0), kcol)),
        pl.BlockSpec((1, tq, kvw), lambda b, i: (b, i, kcol)),
        pl.BlockSpec((1, blk, kvw), lambda b, i: (b, jnp.minimum(i * r + r, nb - 1), kcol)),
        pl.BlockSpec((1, blk, kvw), lambda b, i: (b, jnp.maximum(i * r - 1, 0), vcol)),
        pl.BlockSpec((1, tq, kvw), lambda b, i: (b, i, vcol)),
        pl.BlockSpec((1, blk, kvw), lambda b, i: (b, jnp.minimum(i * r + r, nb - 1), vcol)),
        pl.BlockSpec((seq, HEAD_DIM), lambda b, i: (0, 0)),
        pl.BlockSpec((seq, HEAD_DIM), lambda b, i: (0, 0)),
    ]
    return pl.pallas_call(
        functools.partial(_wattn_kernel, tq=tq, seq=seq),
        grid=(bsz, seq // tq),
        in_specs=in_specs,
        out_specs=pl.BlockSpec((1, tq, qw), lambda b, i: (b, i, 0)),
        out_shape=jax.ShapeDtypeStruct((bsz, seq, qw), BF16),
        scratch_shapes=[pltpu.VMEM((tq, qw), BF16),
                        pltpu.VMEM((tq + 2 * blk, kvw), BF16),
                        pltpu.VMEM((tq + 2 * blk, kvw), BF16)],
        compiler_params=_cparams(("parallel", "parallel")),
    )(sink, proj, proj, proj, proj, proj, proj, proj, cos2, sin2)


def _ssm_prep_kernel(rowp_ref, blre_ref, blim_ref, clre_ref, clim_ref, colp_ref, ccre_ref, ccim_ref, dcol_ref,
                     ktab_ref, win_ref, wout_ref, atab_ref):
    t_len = SSM_CHUNK
    half = SQ // 2
    lane = lax.broadcasted_iota(jnp.int32, (1, SQ), 1)
    is_re_l = (lane % half) < SSM_STATE
    is_f_l = lane < half

    lam_re = rowp_ref[0, 0, 0:1, :]
    lam_im = rowp_ref[0, 0, 1:2, :]
    dt = jnp.exp(rowp_ref[0, 0, 2:3, :])
    re = lam_re * dt
    im = lam_im * dt
    mag = jnp.exp(re)
    n_re = mag * jnp.cos(im) - 1.0
    n_im = mag * jnp.sin(im)
    den = lam_re * lam_re + lam_im * lam_im
    co_re = (n_re * lam_re + n_im * lam_im) / den
    co_im = (n_im * lam_re - n_re * lam_im) / den
    b_re = blre_ref[0, 0]
    b_im = blim_ref[0, 0]
    bb_re = co_re * b_re - co_im * b_im
    bb_im = co_re * b_im + co_im * b_re

    s_i = lax.broadcasted_iota(jnp.int32, (t_len, SQ), 0)
    e = jnp.where(is_f_l, t_len - 1 - s_i, s_i).astype(F32)
    mg = jnp.exp(e * re)
    l_re = mg * jnp.cos(e * im)
    l_im = mg * jnp.sin(e * im)
    p1 = jnp.where(is_re_l, bb_re, bb_im)
    p2 = jnp.where(is_re_l, -bb_im, bb_re)
    for c in range(SSM_GROUP):
        win_ref[0, 0, c * t_len:(c + 1) * t_len, :] = (
            l_re * p1[c:c + 1, :] + l_im * p2[c:c + 1, :]).astype(win_ref.dtype)

    mg_t = jnp.exp(t_len * re)
    a_re = mg_t * jnp.cos(t_len * im)
    a_im = mg_t * jnp.sin(t_len * im)
    atab_ref[0, 0, 0:1, :] = a_re
    atab_ref[0, 0, 1:2, :] = jnp.where(is_re_l, -a_im, a_im)
    atab_ref[0, 0, 2:8, :] = jnp.zeros((6, SQ), F32)

    c_re = clre_ref[0, 0]
    c_im = clim_ref[0, 0]
    blocks = []
    for ci in range(SSM_GROUP):
        br = bb_re[ci:ci + 1, :]
        bi = bb_im[ci:ci + 1, :]
        d_re = c_re * br - c_im * bi
        d_im = c_re * bi + c_im * br
        blocks.append(jnp.where(is_re_l, d_re, -d_im))
    dmat = jnp.concatenate(blocks, axis=0)

    colp = colp_ref[0, 0]
    dt_c = jnp.exp(colp[:, 2:3])
    re_c = colp[:, 0:1] * dt_c
    im_c = colp[:, 1:2] * dt_c
    row = lax.broadcasted_iota(jnp.int32, (SQ, t_len), 0)
    k_i = lax.broadcasted_iota(jnp.int32, (SQ, t_len), 1)
    is_re_c = (row % half) < SSM_STATE
    is_f_c = row < half

    ek = jnp.where(is_f_c, k_i, t_len - k_i).astype(F32)
    mgk = jnp.exp(ek * re_c)
    lt = jnp.where(is_re_c, mgk * jnp.cos(ek * im_c), mgk * jnp.sin(ek * im_c))
    lt = jnp.where(jnp.logical_and(jnp.logical_not(is_f_c), k_i == 0), 0.0, lt)
    k_f = jnp.dot(dmat[:, :half], lt[:half, :], preferred_element_type=F32, precision=lax.Precision.HIGHEST)
    k_b = jnp.dot(dmat[:, half:], lt[half:, :], preferred_element_type=F32, precision=lax.Precision.HIGHEST)
    in_b_re = jnp.logical_and(lane >= half, lane < half + SSM_STATE)
    kb0 = jnp.sum(jnp.where(in_b_re, dmat, 0.0), axis=1, keepdims=True)
    tap0 = lax.broadcasted_iota(jnp.int32, (SQ, t_len), 1) == 0
    k_f = k_f + jnp.where(tap0, kb0 + dcol_ref[0, 0], 0.0)
    ktab_ref[0, 0, :, 0:t_len] = k_f
    ktab_ref[0, 0, :, t_len:2 * t_len] = k_b

    et = jnp.where(is_f_c, k_i + 1, t_len - k_i).astype(F32)
    mgt = jnp.exp(et * re_c)
    l_re2 = mgt * jnp.cos(et * im_c)
    l_im2 = mgt * jnp.sin(et * im_c)
    cc_re = ccre_ref[0, 0]
    cc_im = ccim_ref[0, 0]
    for c in range(SSM_GROUP):
        cr = cc_re[:, c:c + 1]
        ci = cc_im[:, c:c + 1]
        g_re = cr * l_re2 - ci * l_im2
        g_im = cr * l_im2 + ci * l_re2
        wout_ref[0, 0, :, c * t_len:(c + 1) * t_len] = jnp.where(is_re_c, g_re, -g_im).astype(wout_ref.dtype)


def _ssm_prep(lam_re, lam_im, log_dt, b_re, b_im, c_re, c_im, d_skip):
    ns, _, n_g, n_p = lam_re.shape
    cg = SSM_GROUP

    def lanes(a):
        return jnp.concatenate([a[:, 0], a[:, 0], a[:, 1], a[:, 1]], axis=-1)

    ldt = jnp.broadcast_to(log_dt[..., None], lam_re.shape)
    rowp = jnp.stack([lanes(lam_re), lanes(lam_im), lanes(ldt)] + [jnp.zeros((ns, n_g, SQ), F32)] * 5, axis=2)
    colp = jnp.stack([lanes(lam_re), lanes(lam_im), lanes(ldt)], axis=-1)
    bl_re = lanes(jnp.swapaxes(b_re, -1, -2))
    bl_im = lanes(jnp.swapaxes(b_im, -1, -2))
    cl_re = lanes(c_re)
    cl_im = lanes(c_im)
    cc_re = jnp.swapaxes(cl_re, -1, -2)
    cc_im = jnp.swapaxes(cl_im, -1, -2)
    dcol = (jnp.eye(cg, dtype=F32)[None, None] * d_skip.reshape(ns, n_g, 1, cg)).reshape(ns, n_g, cg * cg, 1)

    def spec(*shape):
        nd = len(shape)
        return pl.BlockSpec((1, 1) + shape, lambda j, g: (j, g) + (0,) * nd)

    return pl.pallas_call(
        _ssm_prep_kernel,
        grid=(ns, n_g),
        in_specs=[spec(8, SQ), spec(cg, SQ), spec(cg, SQ), spec(cg, SQ), spec(cg, SQ),
                  spec(SQ, 3), spec(SQ, cg), spec(SQ, cg), spec(cg * cg, 1)],
        out_specs=[spec(cg * cg, 2 * SSM_CHUNK), spec(GW, SQ), spec(SQ, GW), spec(8, SQ)],
        out_shape=[jax.ShapeDtypeStruct((ns, n_g, cg * cg, 2 * SSM_CHUNK), F32),
                   jax.ShapeDtypeStruct((ns, n_g, GW, SQ), BF16),
                   jax.ShapeDtypeStruct((ns, n_g, SQ, GW), BF16),
                   jax.ShapeDtypeStruct((ns, n_g, 8, SQ), F32)],
        compiler_params=_cparams(("parallel", "parallel")),
    )(rowp, bl_re, bl_im, cl_re, cl_im, colp, cc_re, cc_im, dcol)


def _ssm_state_kernel(x_ref, win_ref, s_ref):
    s_ref[0] = jnp.dot(x_ref[0], win_ref[0], preferred_element_type=F32)


def _ssm_state(xg, win):
    n_g, rows, _ = xg.shape
    return pl.pallas_call(
        _ssm_state_kernel,
        grid=(n_g,),
        in_specs=[pl.BlockSpec((1, rows, GW), lambda g: (g, 0, 0)),
                  pl.BlockSpec((1, GW, SQ), lambda g: (g, 0, 0))],
        out_specs=pl.BlockSpec((1, rows, SQ), lambda g: (g, 0, 0)),
        out_shape=jax.ShapeDtypeStruct((n_g, rows, SQ), F32),
        compiler_params=_cparams(("parallel",)),
    )(xg, win)


def _ssm_scan_kernel(s_ref, a_ref, h_ref, *, n_chunks):
    half = SQ // 2
    a1_f = a_ref[0, 0:1, 0:half]
    a2_f = a_ref[0, 1:2, 0:half]
    a1_b = a_ref[0, 0:1, half:]
    a2_b = a_ref[0, 1:2, half:]
    nb = s_ref.shape[2]
    hf = jnp.zeros((nb, half), F32)
    hb = jnp.zeros((nb, half), F32)
    for i in range(n_chunks):
        j = n_chunks - 1 - i
        h_ref[0, i, :, 0:half] = hf.astype(h_ref.dtype)
        h_ref[0, j, :, half:] = hb.astype(h_ref.dtype)
        hf = a1_f * hf + a2_f * pltpu.roll(hf, SSM_STATE, axis=1) + s_ref[0, i, :, 0:half]
        hb = a1_b * hb + a2_b * pltpu.roll(hb, SSM_STATE, axis=1) + s_ref[0, j, :, half:]


def _ssm_scan(s, atab):
    n_g, n_chunks, nb, _ = s.shape
    return pl.pallas_call(
        functools.partial(_ssm_scan_kernel, n_chunks=n_chunks),
        grid=(n_g,),
        in_specs=[pl.BlockSpec((1, n_chunks, nb, SQ), lambda g: (g, 0, 0, 0)),
                  pl.BlockSpec((1, 8, SQ), lambda g: (g, 0, 0))],
        out_specs=pl.BlockSpec((1, n_chunks, nb, SQ), lambda g: (g, 0, 0, 0)),
        out_shape=jax.ShapeDtypeStruct((n_g, n_chunks, nb, SQ), BF16),
        compiler_params=_cparams(("parallel",)),
    )(s, atab)


def _gelu_tanh(y):
    return 0.5 * y * (1.0 + jnp.tanh(math.sqrt(2.0 / math.pi) * (y + 0.044715 * (y * y * y))))


def _ssm_out_kernel(x_ref, h_ref, ktab_ref, wout_ref, z_ref, m_ref):
    t_len = SSM_CHUNK

    def build(ci, carry):
        r0 = pl.multiple_of(ci * t_len, t_len)
        for co in range(SSM_GROUP):
            v = ktab_ref[0, pl.ds(ci * SSM_GROUP + co, 1), :]
            tz = pltpu.roll(jnp.broadcast_to(v, (t_len, 2 * t_len)), 0, axis=1, stride=1, stride_axis=0)
            m_ref[pl.ds(r0, t_len), co * t_len:(co + 1) * t_len] = tz[:, :t_len].astype(m_ref.dtype)
        return carry

    lax.fori_loop(0, SSM_GROUP, build, 0)
    y = jnp.dot(x_ref[0], m_ref[...], preferred_element_type=F32)
    y = y + jnp.dot(h_ref[0], wout_ref[0], preferred_element_type=F32)
    z_ref[0] = _gelu_tanh(y).astype(z_ref.dtype)


def _ssm_out(xg, hprev, ktab, wout):
    n_g, rows, _ = xg.shape
    return pl.pallas_call(
        _ssm_out_kernel,
        grid=(n_g,),
        in_specs=[pl.BlockSpec((1, rows, GW), lambda g: (g, 0, 0)),
                  pl.BlockSpec((1, rows, SQ), lambda g: (g, 0, 0)),
                  pl.BlockSpec((1, SSM_GROUP * SSM_GROUP, 2 * SSM_CHUNK), lambda g: (g, 0, 0)),
                  pl.BlockSpec((1, SQ, GW), lambda g: (g, 0, 0))],
        out_specs=pl.BlockSpec((1, rows, GW), lambda g: (g, 0, 0)),
        out_shape=jax.ShapeDtypeStruct((n_g, rows, GW), BF16),
        scratch_shapes=[pltpu.VMEM((GW, GW), BF16)],
        compiler_params=_cparams(("parallel",)),
    )(xg, hprev, ktab, wout)


def _to_groups(u):
    bsz, seq, width = u.shape
    n_g = width // SSM_GROUP
    nc = seq // SSM_CHUNK
    u = u.reshape(bsz, nc, SSM_CHUNK, n_g, SSM_GROUP)
    return u.transpose(3, 1, 0, 4, 2).reshape(n_g, nc, bsz, GW)


def _from_groups(z, bsz):
    n_g, nc = z.shape[0], z.shape[1]
    z = z.reshape(n_g, nc, bsz, SSM_GROUP, SSM_CHUNK)
    return z.transpose(2, 1, 4, 0, 3).reshape(bsz, nc * SSM_CHUNK, n_g * SSM_GROUP)


def _rope_tables(seq):
    inv_freq = ROPE_THETA ** (-jnp.arange(0, HEAD_DIM, 2, dtype=F32) / HEAD_DIM)
    ang = jnp.arange(seq, dtype=F32)[:, None] * inv_freq[None, :]
    cos, sin = jnp.cos(ang), jnp.sin(ang)
    return jnp.concatenate([cos, cos], axis=-1), jnp.concatenate([-sin, sin], axis=-1)


def kernel(x_prompt, x_sample, mem_prompt, mem_sample, ssm_w_in, ssm_lam_re, ssm_lam_im, ssm_log_dt, ssm_b_re,
           ssm_b_im, ssm_c_re, ssm_c_im, ssm_d, ssm_w_glu, attn_w_in, attn_sink, w_mem_kv, w_out, ln1_g, ln1_b,
           w_ff1, w_ff2, ln2_g, ln2_b):
    depth = w_out.shape[0]
    d_model = x_prompt.shape[-1]
    alpha = (2 * depth) ** 0.25
    mix_w = ssm_d.shape[-1]
    n_groups = mix_w // SSM_GROUP
    mem_w = N_MEM_HEADS * HEAD_DIM
    seqs = (x_prompt.shape[1], x_sample.shape[1])
    assert seqs[0] == seqs[1], "the S5 chunk rows of both request groups are stacked per chunk index"
    n_chunks = seqs[0] // SSM_CHUNK

    ssm_w_in_b = ssm_w_in.astype(BF16)
    ssm_w_glu_b = ssm_w_glu.astype(BF16)
    attn_w_in_b = attn_w_in.astype(BF16)
    w_mem_kv_b = w_mem_kv.astype(BF16)
    w_out_b = w_out.astype(BF16)
    w_ff1_b = w_ff1.astype(BF16)
    w_ff2_b = w_ff2.astype(BF16)

    ktab, win, wout, atab = _ssm_prep(ssm_lam_re, ssm_lam_im, ssm_log_dt, ssm_b_re, ssm_b_im,
                                      ssm_c_re, ssm_c_im, ssm_d.reshape(-1, n_groups, SSM_GROUP))
    cos2, sin2 = _rope_tables(seqs[0])

    xs = [x_prompt.reshape(-1, d_model), x_sample.reshape(-1, d_model)]
    mems = [mem_prompt, mem_sample]
    bszs = [x_prompt.shape[0], x_sample.shape[0]]

    for i in range(depth):
        j = i // 2
        is_ssm = (i % 2 == 0)
        w_in_b = ssm_w_in_b[j] if is_ssm else attn_w_in_b[j]
        projs = [_rowmm(x, w_in_b).reshape(b, seqs[0], -1) for x, b in zip(xs, bszs)]
        kvs = [_rowmm(m.reshape(-1, d_model), w_mem_kv_b[i]).reshape(b, m.shape[1], -1)
               for m, b in zip(mems, bszs)]
        if is_ssm:
            xg = jnp.concatenate([_to_groups(p[..., :mix_w]) for p in projs], axis=2)
            xg = xg.reshape(n_groups, -1, GW)
            s = _ssm_state(xg, win[j])
            hprev = _ssm_scan(s.reshape(n_groups, n_chunks, -1, SQ), atab[j])
            z = _ssm_out(xg, hprev.reshape(n_groups, -1, SQ), ktab[j], wout[j])
            z = z.reshape(n_groups, n_chunks, -1, GW)
            zs = [_from_groups(z[:, :, :bszs[0]], bszs[0]), _from_groups(z[:, :, bszs[0]:], bszs[1])]
            ymix = [_glu(zz.reshape(-1, mix_w), ssm_w_glu_b[j]) for zz in zs]
            qcol = mix_w // mem_w
        else:
            ymix = [_wattn(p, attn_sink[j], cos2, sin2).reshape(-1, mix_w) for p in projs]
            qcol = (mix_w + 2 * N_KV_HEADS * HEAD_DIM) // mem_w
        ymem = [_memattn(p, qcol, kv).reshape(-1, mem_w) for p, kv in zip(projs, kvs)]
        xs = [_oproj(ym, ye, w_out_b[i], x, ln1_g[i], ln1_b[i], alpha)
              for ym, ye, x in zip(ymix, ymem, xs)]
        xs = [_ffn(x, w_ff1_b[i], w_ff2_b[i], ln2_g[i], ln2_b[i], alpha) for x in xs]

    return (xs[0].reshape(x_prompt.shape), xs[1].reshape(x_sample.shape))
```

```python
import functools
import math

import jax
import jax.numpy as jnp
from jax import lax
from jax.experimental import pallas as pl
from jax.experimental.pallas import tpu as pltpu

F32 = jnp.float32
BF16 = jnp.bfloat16

HEAD_DIM = 128
N_KV_HEADS = 4
GQA_GROUP = 3
WINDOW = 128
N_MEM_HEADS = 4
SSM_GROUP = 16
SSM_STATE = 64
SSM_CHUNK = 128
ROPE_THETA = 10000.0
LN_EPS = 1e-5
NEG_INF = -1e30
SQ = 2 * SSM_STATE * 2
GW = SSM_GROUP * SSM_CHUNK
V7X_VMEM_LIMIT_MB = 56


def _cparams(sem, vmem_mb=V7X_VMEM_LIMIT_MB):
    return pltpu.CompilerParams(dimension_semantics=sem, vmem_limit_bytes=vmem_mb * 1024 * 1024)


def _tile(n, target):
    if n <= target:
        return n
    for t in range(target, 7, -1):
        if n % t == 0 and t % 8 == 0:
            return t
    return n


def _layer_norm(y, g, b):
    mu = jnp.mean(y, axis=-1, keepdims=True)
    d = y - mu
    var = jnp.mean(d * d, axis=-1, keepdims=True)
    return d * lax.rsqrt(var + LN_EPS) * g + b


def _rowmm_kernel(x_ref, w_ref, o_ref, *, n_chunk):
    xb = x_ref[...].astype(BF16)
    n = w_ref.shape[1]
    for n0 in range(0, n, n_chunk):
        o_ref[:, n0:n0 + n_chunk] = jnp.dot(
            xb, w_ref[:, n0:n0 + n_chunk], preferred_element_type=F32).astype(o_ref.dtype)


def _rowmm(x, w, tm_target=512):
    m, k = x.shape
    n = w.shape[1]
    tm = _tile(m, tm_target)
    n_chunk = 512 if n % 512 == 0 else n
    return pl.pallas_call(
        functools.partial(_rowmm_kernel, n_chunk=n_chunk),
        grid=(m // tm,),
        in_specs=[pl.BlockSpec((tm, k), lambda i: (i, 0)),
                  pl.BlockSpec((k, n), lambda i: (0, 0))],
        out_specs=pl.BlockSpec((tm, n), lambda i: (i, 0)),
        out_shape=jax.ShapeDtypeStruct((m, n), BF16),
        compiler_params=_cparams(("parallel",)),
    )(x, w)


RELAYOUT_ROWS = 8
CH_PASS = 512
SLAB_PITCH = CH_PASS + 8


def _chunks_per_step(bsz):
    assert RELAYOUT_ROWS % bsz == 0, "batch must divide the 8-row relayout tile"
    return RELAYOUT_ROWS // bsz


def _ssm_inproj_kernel(x_ref, wu_ref, wq_ref, xg_ref, q_ref, tt_ref, *, bsz, cpb):
    d = x_ref.shape[-1]
    xb = x_ref[...].reshape(RELAYOUT_ROWS * SSM_CHUNK, d).astype(BF16)
    q = jnp.dot(xb, wq_ref[...], preferred_element_type=F32)
    q_ref[...] = q.reshape(q_ref.shape).astype(q_ref.dtype)
    for n in range(wu_ref.shape[1] // CH_PASS):
        u = jnp.dot(xb, wu_ref[:, n * CH_PASS:(n + 1) * CH_PASS], preferred_element_type=F32)
        for b in range(bsz):
            for cc in range(cpb):
                src = (b * cpb + cc) * SSM_CHUNK
                slot = cc * bsz + b
                tt_ref[slot * SLAB_PITCH:slot * SLAB_PITCH + CH_PASS, :] = u[src:src + SSM_CHUNK, :].T
        for r in range(CH_PASS):
            g = (n * CH_PASS + r) // SSM_GROUP
            c = r % SSM_GROUP
            xg_ref[g, :, c * SSM_CHUNK:(c + 1) * SSM_CHUNK] = tt_ref[pl.ds(r, RELAYOUT_ROWS, stride=SLAB_PITCH), :]


def _ssm_inproj(x, w_in, mix_w):
    bsz, seq, d = x.shape
    nc = seq // SSM_CHUNK
    cpb = _chunks_per_step(bsz)
    assert nc % cpb == 0 and mix_w % CH_PASS == 0
    n_g = mix_w // SSM_GROUP
    mem_w = w_in.shape[1] - mix_w
    x4 = x.reshape(bsz, nc, SSM_CHUNK, d)
    return pl.pallas_call(
        functools.partial(_ssm_inproj_kernel, bsz=bsz, cpb=cpb),
        grid=(nc // cpb,),
        in_specs=[pl.BlockSpec((bsz, cpb, SSM_CHUNK, d), lambda i: (0, i, 0, 0)),
                  pl.BlockSpec((d, mix_w), lambda i: (0, 0), pipeline_mode=pl.Buffered(1)),
                  pl.BlockSpec((d, mem_w), lambda i: (0, mix_w // mem_w), pipeline_mode=pl.Buffered(1))],
        out_specs=[pl.BlockSpec((n_g, RELAYOUT_ROWS, GW), lambda i: (0, i, 0)),
                   pl.BlockSpec((bsz, cpb * SSM_CHUNK, mem_w), lambda i: (0, i, 0))],
        out_shape=[jax.ShapeDtypeStruct((n_g, nc * bsz, GW), F32),
                   jax.ShapeDtypeStruct((bsz, seq, mem_w), BF16)],
        scratch_shapes=[pltpu.VMEM((RELAYOUT_ROWS * SLAB_PITCH, SSM_CHUNK), F32)],
        compiler_params=_cparams(("parallel",)),
    )(x4, w_in, w_in)


def _glu_kernel(z_ref, wa_ref, wg_ref, o_ref, tt_ref, zt_ref, *, bsz, cpb):
    n_out = wa_ref.shape[1]
    for n in range(zt_ref.shape[1] // CH_PASS):
        for r in range(CH_PASS):
            g = (n * CH_PASS + r) // SSM_GROUP
            c = r % SSM_GROUP
            tt_ref[pl.ds(r, RELAYOUT_ROWS, stride=SLAB_PITCH), :] = z_ref[g, :, c * SSM_CHUNK:(c + 1) * SSM_CHUNK]
        for b in range(bsz):
            for cc in range(cpb):
                dst = (b * cpb + cc) * SSM_CHUNK
                slot = cc * bsz + b
                zt_ref[dst:dst + SSM_CHUNK, n * CH_PASS:(n + 1) * CH_PASS] = (
                    tt_ref[slot * SLAB_PITCH:slot * SLAB_PITCH + CH_PASS, :].T.astype(zt_ref.dtype))
    z = zt_ref[...]
    for n0 in range(0, n_out, CH_PASS):
        a = jnp.dot(z, wa_ref[:, n0:n0 + CH_PASS], preferred_element_type=F32)
        g = jnp.dot(z, wg_ref[:, n0:n0 + CH_PASS], preferred_element_type=F32)
        y = (a * jax.nn.sigmoid(g)).astype(o_ref.dtype)
        o_ref[:, :, n0:n0 + CH_PASS] = y.reshape(bsz, cpb * SSM_CHUNK, CH_PASS)


def _glu(z, w_glu, bsz):
    n_g, rows, _ = z.shape
    mix_w = n_g * SSM_GROUP
    cpb = _chunks_per_step(bsz)
    nc = rows // bsz
    return pl.pallas_call(
        functools.partial(_glu_kernel, bsz=bsz, cpb=cpb),
        grid=(nc // cpb,),
        in_specs=[pl.BlockSpec((n_g, RELAYOUT_ROWS, GW), lambda i: (0, i, 0)),
                  pl.BlockSpec((mix_w, mix_w), lambda i: (0, 0), pipeline_mode=pl.Buffered(1)),
                  pl.BlockSpec((mix_w, mix_w), lambda i: (0, 1), pipeline_mode=pl.Buffered(1))],
        out_specs=pl.BlockSpec((bsz, cpb * SSM_CHUNK, mix_w), lambda i: (0, i, 0)),
        out_shape=jax.ShapeDtypeStruct((bsz, nc * SSM_CHUNK, mix_w), BF16),
        scratch_shapes=[pltpu.VMEM((RELAYOUT_ROWS * SLAB_PITCH, SSM_CHUNK), F32),
                        pltpu.VMEM((RELAYOUT_ROWS * SSM_CHUNK, mix_w), BF16)],
        compiler_params=_cparams(("parallel",)),
    )(z, w_glu, w_glu)


def _oproj_kernel(ymix_ref, ymem_ref, wa_ref, wb_ref, x_ref, g_ref, b_ref, o_ref, *, alpha):
    acc = jnp.dot(ymix_ref[...], wa_ref[...], preferred_element_type=F32)
    acc = acc + jnp.dot(ymem_ref[...], wb_ref[...], preferred_element_type=F32)
    y = alpha * x_ref[...] + acc
    o_ref[...] = _layer_norm(y, g_ref[...], b_ref[...])


def _oproj(ymix, ymem, w_out, x, g, b, alpha, tm_target=512):
    m, d = x.shape
    kmix = ymix.shape[1]
    kmem = ymem.shape[1]
    tm = _tile(m, tm_target)
    return pl.pallas_call(
        functools.partial(_oproj_kernel, alpha=alpha),
        grid=(m // tm,),
        in_specs=[pl.BlockSpec((tm, kmix), lambda i: (i, 0)),
                  pl.BlockSpec((tm, kmem), lambda i: (i, 0)),
                  pl.BlockSpec((kmix, d), lambda i: (0, 0)),
                  pl.BlockSpec((kmem, d), lambda i: (kmix // kmem, 0)),
                  pl.BlockSpec((tm, d), lambda i: (i, 0)),
                  pl.BlockSpec((1, d), lambda i: (0, 0)),
                  pl.BlockSpec((1, d), lambda i: (0, 0))],
        out_specs=pl.BlockSpec((tm, d), lambda i: (i, 0)),
        out_shape=jax.ShapeDtypeStruct((m, d), F32),
        compiler_params=_cparams(("parallel",)),
    )(ymix, ymem, w_out, w_out, x, g.reshape(1, d), b.reshape(1, d))


def _ffn_kernel(x_ref, w1_ref, w2_ref, g_ref, b_ref, o_ref, xb_ref, acc_ref, *, alpha):
    f = pl.program_id(1)

    @pl.when(f == 0)
    def _():
        xb_ref[...] = x_ref[...].astype(BF16)
        acc_ref[...] = jnp.zeros_like(acc_ref)

    h = jnp.dot(xb_ref[...], w1_ref[...], preferred_element_type=F32)
    h = jnp.square(jnp.maximum(h, 0.0)).astype(BF16)
    acc_ref[...] += jnp.dot(h, w2_ref[...], preferred_element_type=F32)

    @pl.when(f == pl.num_programs(1) - 1)
    def _():
        y = alpha * x_ref[...] + acc_ref[...]
        o_ref[...] = _layer_norm(y, g_ref[...], b_ref[...])


def _ffn(x, w1, w2, g, b, alpha, tm_target=512, tf_target=1024):
    m, d = x.shape
    dff = w1.shape[1]
    tm = _tile(m, tm_target)
    tf = _tile(dff, tf_target)
    return pl.pallas_call(
        functools.partial(_ffn_kernel, alpha=alpha),
        grid=(m // tm, dff // tf),
        in_specs=[pl.BlockSpec((tm, d), lambda i, f: (i, 0)),
                  pl.BlockSpec((d, tf), lambda i, f: (0, f)),
                  pl.BlockSpec((tf, d), lambda i, f: (f, 0)),
                  pl.BlockSpec((1, d), lambda i, f: (0, 0)),
                  pl.BlockSpec((1, d), lambda i, f: (0, 0))],
        out_specs=pl.BlockSpec((tm, d), lambda i, f: (i, 0)),
        out_shape=jax.ShapeDtypeStruct((m, d), F32),
        scratch_shapes=[pltpu.VMEM((tm, d), BF16), pltpu.VMEM((tm, d), F32)],
        compiler_params=_cparams(("parallel", "arbitrary")),
    )(x, w1, w2, g.reshape(1, d), b.reshape(1, d))


def _memattn_kernel(q_ref, k_ref, v_ref, o_ref):
    scale = HEAD_DIM ** -0.5
    for h in range(N_MEM_HEADS):
        sl = slice(h * HEAD_DIM, (h + 1) * HEAD_DIM)
        s = lax.dot_general(q_ref[0, :, sl], k_ref[0, :, sl], (((1,), (1,)), ((), ())),
                            preferred_element_type=F32) * scale
        m = jnp.max(s, axis=-1, keepdims=True)
        p = jnp.exp(s - m)
        l = jnp.sum(p, axis=-1, keepdims=True)
        o = jnp.dot(p.astype(BF16), v_ref[0, :, sl], preferred_element_type=F32)
        o_ref[0, :, sl] = (o / l).astype(o_ref.dtype)


def _memattn(proj, q_col_block, kv, tq_target=1024):
    bsz, seq, _ = proj.shape
    n_mem = kv.shape[1]
    w = N_MEM_HEADS * HEAD_DIM
    tq = _tile(seq, tq_target)
    return pl.pallas_call(
        _memattn_kernel,
        grid=(bsz, seq // tq),
        in_specs=[pl.BlockSpec((1, tq, w), lambda b, i: (b, i, q_col_block)),
                  pl.BlockSpec((1, n_mem, w), lambda b, i: (b, 0, 0)),
                  pl.BlockSpec((1, n_mem, w), lambda b, i: (b, 0, 1))],
        out_specs=pl.BlockSpec((1, tq, w), lambda b, i: (b, i, 0)),
        out_shape=jax.ShapeDtypeStruct((bsz, seq, w), BF16),
        compiler_params=_cparams(("parallel", "parallel")),
    )(proj, kv, kv)


def _rope(x, cos2, sin2):
    return x * cos2 + pltpu.roll(x, HEAD_DIM // 2, axis=1) * sin2


def _wattn_kernel(sink_ref, q_ref, kp_ref, kc_ref, kn_ref, vp_ref, vc_ref, vn_ref, cos_ref, sin_ref,
                  o_ref, qr_ref, kr_ref, vr_ref, *, tq, seq):
    i = pl.program_id(1)
    base = i * tq
    blk = WINDOW
    nqb = tq // blk
    scale = HEAD_DIM ** -0.5
    n_q_heads = N_KV_HEADS * GQA_GROUP

    cos_c = cos_ref[pl.ds(pl.multiple_of(base, blk), tq), :]
    sin_c = sin_ref[pl.ds(pl.multiple_of(base, blk), tq), :]
    p_start = pl.multiple_of(jnp.maximum(base - blk, 0), blk)
    n_start = pl.multiple_of(jnp.minimum(base + tq, seq - blk), blk)
    cos_p = cos_ref[pl.ds(p_start, blk), :]
    sin_p = sin_ref[pl.ds(p_start, blk), :]
    cos_n = cos_ref[pl.ds(n_start, blk), :]
    sin_n = sin_ref[pl.ds(n_start, blk), :]

    for h in range(n_q_heads):
        sl = slice(h * HEAD_DIM, (h + 1) * HEAD_DIM)
        qr_ref[:, sl] = (_rope(q_ref[0, :, sl].astype(F32), cos_c, sin_c) * scale).astype(BF16)
    for h in range(N_KV_HEADS):
        sl = slice(h * HEAD_DIM, (h + 1) * HEAD_DIM)
        kr_ref[0:blk, sl] = _rope(kp_ref[0, :, sl].astype(F32), cos_p, sin_p).astype(BF16)
        kr_ref[blk:blk + tq, sl] = _rope(kc_ref[0, :, sl].astype(F32), cos_c, sin_c).astype(BF16)
        kr_ref[blk + tq:, sl] = _rope(kn_ref[0, :, sl].astype(F32), cos_n, sin_n).astype(BF16)
    vr_ref[0:blk, :] = vp_ref[0]
    vr_ref[blk:blk + tq, :] = vc_ref[0]
    vr_ref[blk + tq:, :] = vn_ref[0]

    r = lax.broadcasted_iota(jnp.int32, (blk, 3 * blk), 0)
    c = lax.broadcasted_iota(jnp.int32, (blk, 3 * blk), 1)
    rel = c - blk - r
    band = jnp.abs(rel) <= WINDOW

    for jb in range(nqb):
        kpos = base + (jb - 1) * blk + c
        valid = band & (kpos >= 0) & (kpos < seq)
        bias = jnp.where(valid, 0.0, NEG_INF).astype(F32)
        for hk in range(N_KV_HEADS):
            ksl = slice(hk * HEAD_DIM, (hk + 1) * HEAD_DIM)
            kw = kr_ref[jb * blk:(jb + 3) * blk, ksl]
            vw = vr_ref[jb * blk:(jb + 3) * blk, ksl]
            qs = jnp.concatenate(
                [qr_ref[jb * blk:(jb + 1) * blk, (hk * GQA_GROUP + g) * HEAD_DIM:(hk * GQA_GROUP + g + 1) * HEAD_DIM]
                 for g in range(GQA_GROUP)], axis=0)
            s = lax.dot_general(qs, kw, (((1,), (1,)), ((), ())), preferred_element_type=F32)
            ps = []
            inv = []
            for g in range(GQA_GROUP):
                sk = sink_ref[hk * GQA_GROUP + g]
                sg = s[g * blk:(g + 1) * blk, :] + bias
                m = jnp.maximum(jnp.max(sg, axis=-1, keepdims=True), sk)
                p = jnp.exp(sg - m)
                den = jnp.sum(p, axis=-1, keepdims=True) + jnp.exp(sk - m)
                ps.append(p.astype(BF16))
                inv.append(1.0 / den)
            o = jnp.dot(jnp.concatenate(ps, axis=0), vw, preferred_element_type=F32)
            for g in range(GQA_GROUP):
                hq = hk * GQA_GROUP + g
                o_ref[0, jb * blk:(jb + 1) * blk, hq * HEAD_DIM:(hq + 1) * HEAD_DIM] = (
                    o[g * blk:(g + 1) * blk, :] * inv[g]).astype(o_ref.dtype)


def _wattn(proj, sink, cos2, sin2, tq_target=512):
    bsz, seq, _ = proj.shape
    blk = WINDOW
    qw = N_KV_HEADS * GQA_GROUP * HEAD_DIM
    kvw = N_KV_HEADS * HEAD_DIM
    tq = _tile(seq, tq_target)
    r = tq // blk
    nb = seq // blk
    kcol = qw // kvw
    vcol = kcol + 1
    in_specs = [
        pl.BlockSpec(memory_space=pltpu.SMEM),
        pl.BlockSpec((1, tq, qw), lambda b, i: (b, i, 0)),
        pl.BlockSpec((1, blk, kvw), lambda b, i: (b, jnp.maximum(i * r - 1, 0), kcol)),
        pl.BlockSpec((1, tq, kvw), lambda b, i: (b, i, kcol)),
        pl.BlockSpec((1, blk, kvw), lambda b, i: (b, jnp.minimum(i * r + r, nb - 1), kcol)),
        pl.BlockSpec((1, blk, kvw), lambda b, i: (b, jnp.maximum(i * r - 1, 0), vcol)),
        pl.BlockSpec((1, tq, kvw), lambda b, i: (b, i, vcol)),
        pl.BlockSpec((1, blk, kvw), lambda b, i: (b, jnp.minimum(i * r + r, nb - 1), vcol)),
        pl.BlockSpec((seq, HEAD_DIM), lambda b, i: (0, 0)),
        pl.BlockSpec((seq, HEAD_DIM), lambda b, i: (0, 0)),
    ]
    return pl.pallas_call(
        functools.partial(_wattn_kernel, tq=tq, seq=seq),
        grid=(bsz, seq // tq),
        in_specs=in_specs,
        out_specs=pl.BlockSpec((1, tq, qw), lambda b, i: (b, i, 0)),
        out_shape=jax.ShapeDtypeStruct((bsz, seq, qw), BF16),
        scratch_shapes=[pltpu.VMEM((tq, qw), BF16),
                        pltpu.VMEM((tq + 2 * blk, kvw), BF16),
                        pltpu.VMEM((tq + 2 * blk, kvw), BF16)],
        compiler_params=_cparams(("parallel", "parallel")),
    )(sink, proj, proj, proj, proj, proj, proj, proj, cos2, sin2)


def _ssm_prep_kernel(rowp_ref, blre_ref, blim_ref, clre_ref, clim_ref, colp_ref, ccre_ref, ccim_ref, dcol_ref,
                     ktab_ref, win_ref, wout_ref, atab_ref):
    t_len = SSM_CHUNK
    half = SQ // 2
    lane = lax.broadcasted_iota(jnp.int32, (1, SQ), 1)
    is_re_l = (lane % half) < SSM_STATE
    is_f_l = lane < half

    lam_re = rowp_ref[0, 0, 0:1, :]
    lam_im = rowp_ref[0, 0, 1:2, :]
    dt = jnp.exp(rowp_ref[0, 0, 2:3, :])
    re = lam_re * dt
    im = lam_im * dt
    mag = jnp.exp(re)
    n_re = mag * jnp.cos(im) - 1.0
    n_im = mag * jnp.sin(im)
    den = lam_re * lam_re + lam_im * lam_im
    co_re = (n_re * lam_re + n_im * lam_im) / den
    co_im = (n_im * lam_re - n_re * lam_im) / den
    b_re = blre_ref[0, 0]
    b_im = blim_ref[0, 0]
    bb_re = co_re * b_re - co_im * b_im
    bb_im = co_re * b_im + co_im * b_re

    s_i = lax.broadcasted_iota(jnp.int32, (t_len, SQ), 0)
    e = jnp.where(is_f_l, t_len - 1 - s_i, s_i).astype(F32)
    mg = jnp.exp(e * re)
    l_re = mg * jnp.cos(e * im)
    l_im = mg * jnp.sin(e * im)
    p1 = jnp.where(is_re_l, bb_re, bb_im)
    p2 = jnp.where(is_re_l, -bb_im, bb_re)
    for c in range(SSM_GROUP):
        win_ref[0, 0, c * t_len:(c + 1) * t_len, :] = (
            l_re * p1[c:c + 1, :] + l_im * p2[c:c + 1, :]).astype(win_ref.dtype)

    mg_t = jnp.exp(t_len * re)
    a_re = mg_t * jnp.cos(t_len * im)
    a_im = mg_t * jnp.sin(t_len * im)
    atab_ref[0, 0, 0:1, :] = a_re
    atab_ref[0, 0, 1:2, :] = jnp.where(is_re_l, -a_im, a_im)
    atab_ref[0, 0, 2:8, :] = jnp.zeros((6, SQ), F32)

    c_re = clre_ref[0, 0]
    c_im = clim_ref[0, 0]
    blocks = []
    for ci in range(SSM_GROUP):
        br = bb_re[ci:ci + 1, :]
        bi = bb_im[ci:ci + 1, :]
        d_re = c_re * br - c_im * bi
        d_im = c_re * bi + c_im * br
        blocks.append(jnp.where(is_re_l, d_re, -d_im))
    dmat = jnp.concatenate(blocks, axis=0)

    colp = colp_ref[0, 0]
    dt_c = jnp.exp(colp[:, 2:3])
    re_c = colp[:, 0:1] * dt_c
    im_c = colp[:, 1:2] * dt_c
    row = lax.broadcasted_iota(jnp.int32, (SQ, t_len), 0)
    k_i = lax.broadcasted_iota(jnp.int32, (SQ, t_len), 1)
    is_re_c = (row % half) < SSM_STATE
    is_f_c = row < half

    ek = jnp.where(is_f_c, k_i, t_len - k_i).astype(F32)
    mgk = jnp.exp(ek * re_c)
    lt = jnp.where(is_re_c, mgk * jnp.cos(ek * im_c), mgk * jnp.sin(ek * im_c))
    lt = jnp.where(jnp.logical_and(jnp.logical_not(is_f_c), k_i == 0), 0.0, lt)
    k_f = jnp.dot(dmat[:, :half], lt[:half, :], preferred_element_type=F32, precision=lax.Precision.HIGHEST)
    k_b = jnp.dot(dmat[:, half:], lt[half:, :], preferred_element_type=F32, precision=lax.Precision.HIGHEST)
    in_b_re = jnp.logical_and(lane >= half, lane < half + SSM_STATE)
    kb0 = jnp.sum(jnp.where(in_b_re, dmat, 0.0), axis=1, keepdims=True)
    tap0 = lax.broadcasted_iota(jnp.int32, (SQ, t_len), 1) == 0
    k_f = k_f + jnp.where(tap0, kb0 + dcol_ref[0, 0], 0.0)
    ktab_ref[0, 0, :, 0:t_len] = k_f
    ktab_ref[0, 0, :, t_len:2 * t_len] = k_b

    et = jnp.where(is_f_c, k_i + 1, t_len - k_i).astype(F32)
    mgt = jnp.exp(et * re_c)
    l_re2 = mgt * jnp.cos(et * im_c)
    l_im2 = mgt * jnp.sin(et * im_c)
    cc_re = ccre_ref[0, 0]
    cc_im = ccim_ref[0, 0]
    for c in range(SSM_GROUP):
        cr = cc_re[:, c:c + 1]
        ci = cc_im[:, c:c + 1]
        g_re = cr * l_re2 - ci * l_im2
        g_im = cr * l_im2 + ci * l_re2
        wout_ref[0, 0, :, c * t_len:(c + 1) * t_len] = jnp.where(is_re_c, g_re, -g_im).astype(wout_ref.dtype)


def _ssm_prep(lam_re, lam_im, log_dt, b_re, b_im, c_re, c_im, d_skip):
    ns, _, n_g, n_p = lam_re.shape
    cg = SSM_GROUP

    def lanes(a):
        return jnp.concatenate([a[:, 0], a[:, 0], a[:, 1], a[:, 1]], axis=-1)

    ldt = jnp.broadcast_to(log_dt[..., None], lam_re.shape)
    rowp = jnp.stack([lanes(lam_re), lanes(lam_im), lanes(ldt)] + [jnp.zeros((ns, n_g, SQ), F32)] * 5, axis=2)
    colp = jnp.stack([lanes(lam_re), lanes(lam_im), lanes(ldt)], axis=-1)
    bl_re = lanes(jnp.swapaxes(b_re, -1, -2))
    bl_im = lanes(jnp.swapaxes(b_im, -1, -2))
    cl_re = lanes(c_re)
    cl_im = lanes(c_im)
    cc_re = jnp.swapaxes(cl_re, -1, -2)
    cc_im = jnp.swapaxes(cl_im, -1, -2)
    dcol = (jnp.eye(cg, dtype=F32)[None, None] * d_skip.reshape(ns, n_g, 1, cg)).reshape(ns, n_g, cg * cg, 1)

    def spec(*shape):
        nd = len(shape)
        return pl.BlockSpec((1, 1) + shape, lambda j, g: (j, g) + (0,) * nd)

    return pl.pallas_call(
        _ssm_prep_kernel,
        grid=(ns, n_g),
        in_specs=[spec(8, SQ), spec(cg, SQ), spec(cg, SQ), spec(cg, SQ), spec(cg, SQ),
                  spec(SQ, 3), spec(SQ, cg), spec(SQ, cg), spec(cg * cg, 1)],
        out_specs=[spec(cg * cg, 2 * SSM_CHUNK), spec(GW, SQ), spec(SQ, GW), spec(8, SQ)],
        out_shape=[jax.ShapeDtypeStruct((ns, n_g, cg * cg, 2 * SSM_CHUNK), F32),
                   jax.ShapeDtypeStruct((ns, n_g, GW, SQ), BF16),
                   jax.ShapeDtypeStruct((ns, n_g, SQ, GW), BF16),
                   jax.ShapeDtypeStruct((ns, n_g, 8, SQ), F32)],
        compiler_params=_cparams(("parallel", "parallel")),
    )(rowp, bl_re, bl_im, cl_re, cl_im, colp, cc_re, cc_im, dcol)


def _ssm_state_kernel(*refs, n_in):
    win_ref = refs[n_in]
    for x_ref, s_ref in zip(refs[:n_in], refs[n_in + 1:]):
        s_ref[0] = jnp.dot(x_ref[0].astype(BF16), win_ref[0], preferred_element_type=F32)


def _ssm_state(xgs, win):
    n_g = win.shape[0]
    n_in = len(xgs)
    return pl.pallas_call(
        functools.partial(_ssm_state_kernel, n_in=n_in),
        grid=(n_g,),
        in_specs=[pl.BlockSpec((1, xg.shape[1], GW), lambda g: (g, 0, 0)) for xg in xgs]
        + [pl.BlockSpec((1, GW, SQ), lambda g: (g, 0, 0))],
        out_specs=[pl.BlockSpec((1, xg.shape[1], SQ), lambda g: (g, 0, 0)) for xg in xgs],
        out_shape=[jax.ShapeDtypeStruct((n_g, xg.shape[1], SQ), F32) for xg in xgs],
        compiler_params=_cparams(("parallel",)),
    )(*xgs, win)


def _ssm_scan_kernel(s_ref, a_ref, h_ref, *, n_chunks):
    half = SQ // 2
    a1_f = a_ref[0, 0:1, 0:half]
    a2_f = a_ref[0, 1:2, 0:half]
    a1_b = a_ref[0, 0:1, half:]
    a2_b = a_ref[0, 1:2, half:]
    nb = s_ref.shape[2]
    hf = jnp.zeros((nb, half), F32)
    hb = jnp.zeros((nb, half), F32)
    for i in range(n_chunks):
        j = n_chunks - 1 - i
        h_ref[0, i, :, 0:half] = hf.astype(h_ref.dtype)
        h_ref[0, j, :, half:] = hb.astype(h_ref.dtype)
        hf = a1_f * hf + a2_f * pltpu.roll(hf, SSM_STATE, axis=1) + s_ref[0, i, :, 0:half]
        hb = a1_b * hb + a2_b * pltpu.roll(hb, SSM_STATE, axis=1) + s_ref[0, j, :, half:]


def _ssm_scan(s, atab):
    n_g, n_chunks, nb, _ = s.shape
    return pl.pallas_call(
        functools.partial(_ssm_scan_kernel, n_chunks=n_chunks),
        grid=(n_g,),
        in_specs=[pl.BlockSpec((1, n_chunks, nb, SQ), lambda g: (g, 0, 0, 0)),
                  pl.BlockSpec((1, 8, SQ), lambda g: (g, 0, 0))],
        out_specs=pl.BlockSpec((1, n_chunks, nb, SQ), lambda g: (g, 0, 0, 0)),
        out_shape=jax.ShapeDtypeStruct((n_g, n_chunks, nb, SQ), BF16),
        compiler_params=_cparams(("parallel",)),
    )(s, atab)


def _gelu_tanh(y):
    return 0.5 * y * (1.0 + jnp.tanh(math.sqrt(2.0 / math.pi) * (y + 0.044715 * (y * y * y))))


def _ssm_out_kernel(*refs, n_in):
    x_refs = refs[:n_in]
    h_refs = refs[n_in:2 * n_in]
    ktab_ref, wout_ref = refs[2 * n_in:2 * n_in + 2]
    z_refs = refs[2 * n_in + 2:3 * n_in + 2]
    m_ref = refs[3 * n_in + 2]
    t_len = SSM_CHUNK

    def build(ci, carry):
        r0 = pl.multiple_of(ci * t_len, t_len)
        for co in range(SSM_GROUP):
            v = ktab_ref[0, pl.ds(ci * SSM_GROUP + co, 1), :]
            tz = pltpu.roll(jnp.broadcast_to(v, (t_len, 2 * t_len)), 0, axis=1, stride=1, stride_axis=0)
            m_ref[pl.ds(r0, t_len), co * t_len:(co + 1) * t_len] = tz[:, :t_len].astype(m_ref.dtype)
        return carry

    lax.fori_loop(0, SSM_GROUP, build, 0)
    for x_ref, h_ref, z_ref in zip(x_refs, h_refs, z_refs):
        y = jnp.dot(x_ref[0].astype(BF16), m_ref[...], preferred_element_type=F32)
        y = y + jnp.dot(h_ref[0], wout_ref[0], preferred_element_type=F32)
        z_ref[0] = _gelu_tanh(y).astype(z_ref.dtype)


def _ssm_out(xgs, hprevs, ktab, wout):
    n_g = ktab.shape[0]
    n_in = len(xgs)
    return pl.pallas_call(
        functools.partial(_ssm_out_kernel, n_in=n_in),
        grid=(n_g,),
        in_specs=[pl.BlockSpec((1, xg.shape[1], GW), lambda g: (g, 0, 0)) for xg in xgs]
        + [pl.BlockSpec((1, h.shape[1], SQ), lambda g: (g, 0, 0)) for h in hprevs]
        + [pl.BlockSpec((1, SSM_GROUP * SSM_GROUP, 2 * SSM_CHUNK), lambda g: (g, 0, 0)),
           pl.BlockSpec((1, SQ, GW), lambda g: (g, 0, 0))],
        out_specs=[pl.BlockSpec((1, xg.shape[1], GW), lambda g: (g, 0, 0)) for xg in xgs],
        out_shape=[jax.ShapeDtypeStruct((n_g, xg.shape[1], GW), F32) for xg in xgs],
        scratch_shapes=[pltpu.VMEM((GW, GW), BF16)],
        compiler_params=_cparams(("parallel",)),
    )(*xgs, *hprevs, ktab, wout)


def _rope_tables(seq):
    inv_freq = ROPE_THETA ** (-jnp.arange(0, HEAD_DIM, 2, dtype=F32) / HEAD_DIM)
    ang = jnp.arange(seq, dtype=F32)[:, None] * inv_freq[None, :]
    cos, sin = jnp.cos(ang), jnp.sin(ang)
    return jnp.concatenate([cos, cos], axis=-1), jnp.concatenate([-sin, sin], axis=-1)


def kernel(x_prompt, x_sample, mem_prompt, mem_sample, ssm_w_in, ssm_lam_re, ssm_lam_im, ssm_log_dt, ssm_b_re,
           ssm_b_im, ssm_c_re, ssm_c_im, ssm_d, ssm_w_glu, attn_w_in, attn_sink, w_mem_kv, w_out, ln1_g, ln1_b,
           w_ff1, w_ff2, ln2_g, ln2_b):
    depth = w_out.shape[0]
    d_model = x_prompt.shape[-1]
    alpha = (2 * depth) ** 0.25
    mix_w = ssm_d.shape[-1]
    n_groups = mix_w // SSM_GROUP
    mem_w = N_MEM_HEADS * HEAD_DIM

    ssm_w_in_b = ssm_w_in.astype(BF16)
    ssm_w_glu_b = ssm_w_glu.astype(BF16)
    attn_w_in_b = attn_w_in.astype(BF16)
    w_mem_kv_b = w_mem_kv.astype(BF16)
    w_out_b = w_out.astype(BF16)
    w_ff1_b = w_ff1.astype(BF16)
    w_ff2_b = w_ff2.astype(BF16)

    ktab, win, wout, atab = _ssm_prep(ssm_lam_re, ssm_lam_im, ssm_log_dt, ssm_b_re, ssm_b_im,
                                      ssm_c_re, ssm_c_im, ssm_d.reshape(-1, n_groups, SSM_GROUP))
    shapes = [x_prompt.shape, x_sample.shape]
    ropes = [_rope_tables(s[1]) for s in shapes]
    xs = [x_prompt.reshape(-1, d_model), x_sample.reshape(-1, d_model)]
    mems = [mem_prompt, mem_sample]

    for i in range(depth):
        j = i // 2
        kvs = [_rowmm(m.reshape(-1, d_model), w_mem_kv_b[i]).reshape(m.shape[0], m.shape[1], -1) for m in mems]
        if i % 2 == 0:
            xgs, qms = zip(*[_ssm_inproj(x.reshape(s), ssm_w_in_b[j], mix_w) for x, s in zip(xs, shapes)])
            ss = _ssm_state(xgs, win[j])
            hprevs = [_ssm_scan(s.reshape(n_groups, -1, shp[0], SQ), atab[j]).reshape(n_groups, -1, SQ)
                      for s, shp in zip(ss, shapes)]
            zs = _ssm_out(xgs, hprevs, ktab[j], wout[j])
            ymix = [_glu(z, ssm_w_glu_b[j], shp[0]) for z, shp in zip(zs, shapes)]
            ymem = [_memattn(q, 0, kv) for q, kv in zip(qms, kvs)]
        else:
            projs = [_rowmm(x, attn_w_in_b[j]).reshape(s[0], s[1], -1) for x, s in zip(xs, shapes)]
            ymix = [_wattn(p, attn_sink[j], cos2, sin2) for p, (cos2, sin2) in zip(projs, ropes)]
            qcol = (mix_w + 2 * N_KV_HEADS * HEAD_DIM) // mem_w
            ymem = [_memattn(p, qcol, kv) for p, kv in zip(projs, kvs)]
        xs = [_oproj(ym.reshape(-1, mix_w), ye.reshape(-1, mem_w), w_out_b[i], x, ln1_g[i], ln1_b[i], alpha)
              for ym, ye, x in zip(ymix, ymem, xs)]
        xs = [_ffn(x, w_ff1_b[i], w_ff2_b[i], ln2_g[i], ln2_b[i], alpha) for x in xs]

    return (xs[0].reshape(shapes[0]), xs[1].reshape(shapes[1]))
```

```python
import functools
import math

import jax
import jax.numpy as jnp
from jax import lax
from jax.experimental import pallas as pl
from jax.experimental.pallas import tpu as pltpu

F32 = jnp.float32
BF16 = jnp.bfloat16

HEAD_DIM = 128
N_KV_HEADS = 4
GQA_GROUP = 3
WINDOW = 128
N_MEM_HEADS = 4
SSM_GROUP = 16
SSM_STATE = 64
SSM_CHUNK = 128
ROPE_THETA = 10000.0
LN_EPS = 1e-5
NEG_INF = -1e30
SQ = 2 * SSM_STATE * 2
GW = SSM_GROUP * SSM_CHUNK
V7X_VMEM_LIMIT_MB = 56


def _cparams(sem, vmem_mb=V7X_VMEM_LIMIT_MB):
    return pltpu.CompilerParams(dimension_semantics=sem, vmem_limit_bytes=vmem_mb * 1024 * 1024)


def _tile(n, target):
    if n <= target:
        return n
    for t in range(target, 7, -1):
        if n % t == 0 and t % 8 == 0:
            return t
    return n


def _layer_norm(y, g, b):
    mu = jnp.mean(y, axis=-1, keepdims=True)
    d = y - mu
    var = jnp.mean(d * d, axis=-1, keepdims=True)
    return d * lax.rsqrt(var + LN_EPS) * g + b


def _rowmm_kernel(x_ref, w_ref, o_ref, *, n_chunk):
    xb = x_ref[...].astype(BF16)
    n = w_ref.shape[1]
    for n0 in range(0, n, n_chunk):
        o_ref[:, n0:n0 + n_chunk] = jnp.dot(
            xb, w_ref[:, n0:n0 + n_chunk], preferred_element_type=F32).astype(o_ref.dtype)


def _rowmm(x, w, layer, tm_target=512):
    m, k = x.shape
    n = w.shape[2]
    tm = _tile(m, tm_target)
    n_chunk = 512 if n % 512 == 0 else n
    return pl.pallas_call(
        functools.partial(_rowmm_kernel, n_chunk=n_chunk),
        grid=(m // tm,),
        in_specs=[pl.BlockSpec((tm, k), lambda i: (i, 0)),
                  pl.BlockSpec((None, k, n), lambda i: (layer, 0, 0), pipeline_mode=pl.Buffered(1))],
        out_specs=pl.BlockSpec((tm, n), lambda i: (i, 0)),
        out_shape=jax.ShapeDtypeStruct((m, n), BF16),
        compiler_params=_cparams(("parallel",)),
    )(x, w)


RELAYOUT_ROWS = 8
CH_PASS = 512
SLAB_PITCH = CH_PASS + 8


def _chunks_per_step(bsz):
    assert RELAYOUT_ROWS % bsz == 0, "batch must divide the 8-row relayout tile"
    return RELAYOUT_ROWS // bsz


def _ssm_inproj_kernel(x_ref, wu_ref, wq_ref, xg_ref, q_ref, tt_ref, *, bsz, cpb):
    d = x_ref.shape[-1]
    xb = x_ref[...].reshape(RELAYOUT_ROWS * SSM_CHUNK, d).astype(BF16)
    q = jnp.dot(xb, wq_ref[...], preferred_element_type=F32)
    q_ref[...] = q.reshape(q_ref.shape).astype(q_ref.dtype)
    for n in range(wu_ref.shape[1] // CH_PASS):
        u = jnp.dot(xb, wu_ref[:, n * CH_PASS:(n + 1) * CH_PASS], preferred_element_type=F32)
        for b in range(bsz):
            for cc in range(cpb):
                src = (b * cpb + cc) * SSM_CHUNK
                slot = cc * bsz + b
                tt_ref[slot * SLAB_PITCH:slot * SLAB_PITCH + CH_PASS, :] = u[src:src + SSM_CHUNK, :].T
        for r in range(CH_PASS):
            g = (n * CH_PASS + r) // SSM_GROUP
            c = r % SSM_GROUP
            xg_ref[g, :, c * SSM_CHUNK:(c + 1) * SSM_CHUNK] = tt_ref[pl.ds(r, RELAYOUT_ROWS, stride=SLAB_PITCH), :]


def _ssm_inproj(x, w_in, layer, mix_w):
    bsz, seq, d = x.shape
    nc = seq // SSM_CHUNK
    cpb = _chunks_per_step(bsz)
    assert nc % cpb == 0 and mix_w % CH_PASS == 0
    n_g = mix_w // SSM_GROUP
    mem_w = w_in.shape[2] - mix_w
    x4 = x.reshape(bsz, nc, SSM_CHUNK, d)
    return pl.pallas_call(
        functools.partial(_ssm_inproj_kernel, bsz=bsz, cpb=cpb),
        grid=(nc // cpb,),
        in_specs=[pl.BlockSpec((bsz, cpb, SSM_CHUNK, d), lambda i: (0, i, 0, 0)),
                  pl.BlockSpec((None, d, mix_w), lambda i: (layer, 0, 0), pipeline_mode=pl.Buffered(1)),
                  pl.BlockSpec((None, d, mem_w), lambda i: (layer, 0, mix_w // mem_w),
                               pipeline_mode=pl.Buffered(1))],
        out_specs=[pl.BlockSpec((n_g, RELAYOUT_ROWS, GW), lambda i: (0, i, 0)),
                   pl.BlockSpec((bsz, cpb * SSM_CHUNK, mem_w), lambda i: (0, i, 0))],
        out_shape=[jax.ShapeDtypeStruct((n_g, nc * bsz, GW), F32),
                   jax.ShapeDtypeStruct((bsz, seq, mem_w), BF16)],
        scratch_shapes=[pltpu.VMEM((RELAYOUT_ROWS * SLAB_PITCH, SSM_CHUNK), F32)],
        compiler_params=_cparams(("parallel",)),
    )(x4, w_in, w_in)


def _glu_kernel(z_ref, wa_ref, wg_ref, o_ref, tt_ref, zt_ref, *, bsz, cpb):
    n_out = wa_ref.shape[1]
    for n in range(zt_ref.shape[1] // CH_PASS):
        for r in range(CH_PASS):
            g = (n * CH_PASS + r) // SSM_GROUP
            c = r % SSM_GROUP
            tt_ref[pl.ds(r, RELAYOUT_ROWS, stride=SLAB_PITCH), :] = z_ref[g, :, c * SSM_CHUNK:(c + 1) * SSM_CHUNK]
        for b in range(bsz):
            for cc in range(cpb):
                dst = (b * cpb + cc) * SSM_CHUNK
                slot = cc * bsz + b
                zt_ref[dst:dst + SSM_CHUNK, n * CH_PASS:(n + 1) * CH_PASS] = (
                    tt_ref[slot * SLAB_PITCH:slot * SLAB_PITCH + CH_PASS, :].T.astype(zt_ref.dtype))
    z = zt_ref[...]
    for n0 in range(0, n_out, CH_PASS):
        a = jnp.dot(z, wa_ref[:, n0:n0 + CH_PASS], preferred_element_type=F32)
        g = jnp.dot(z, wg_ref[:, n0:n0 + CH_PASS], preferred_element_type=F32)
        y = (a * jax.nn.sigmoid(g)).astype(o_ref.dtype)
        o_ref[:, :, n0:n0 + CH_PASS] = y.reshape(bsz, cpb * SSM_CHUNK, CH_PASS)


def _glu(z, w_glu, layer, bsz):
    n_g, rows, _ = z.shape
    mix_w = n_g * SSM_GROUP
    cpb = _chunks_per_step(bsz)
    nc = rows // bsz
    return pl.pallas_call(
        functools.partial(_glu_kernel, bsz=bsz, cpb=cpb),
        grid=(nc // cpb,),
        in_specs=[pl.BlockSpec((n_g, RELAYOUT_ROWS, GW), lambda i: (0, i, 0)),
                  pl.BlockSpec((None, mix_w, mix_w), lambda i: (layer, 0, 0), pipeline_mode=pl.Buffered(1)),
                  pl.BlockSpec((None, mix_w, mix_w), lambda i: (layer, 0, 1), pipeline_mode=pl.Buffered(1))],
        out_specs=pl.BlockSpec((bsz, cpb * SSM_CHUNK, mix_w), lambda i: (0, i, 0)),
        out_shape=jax.ShapeDtypeStruct((bsz, nc * SSM_CHUNK, mix_w), BF16),
        scratch_shapes=[pltpu.VMEM((RELAYOUT_ROWS * SLAB_PITCH, SSM_CHUNK), F32),
                        pltpu.VMEM((RELAYOUT_ROWS * SSM_CHUNK, mix_w), BF16)],
        compiler_params=_cparams(("parallel",)),
    )(z, w_glu, w_glu)


def _oproj_kernel(ymix_ref, ymem_ref, wa_ref, wb_ref, x_ref, g_ref, b_ref, o_ref, *, alpha):
    acc = jnp.dot(ymix_ref[...], wa_ref[...], preferred_element_type=F32)
    acc = acc + jnp.dot(ymem_ref[...], wb_ref[...], preferred_element_type=F32)
    y = alpha * x_ref[...] + acc
    o_ref[...] = _layer_norm(y, g_ref[...], b_ref[...])


def _oproj(ymix, ymem, w_out, x, g, b, layer, alpha, tm_target=512):
    m, d = x.shape
    kmix = ymix.shape[1]
    kmem = ymem.shape[1]
    tm = _tile(m, tm_target)
    return pl.pallas_call(
        functools.partial(_oproj_kernel, alpha=alpha),
        grid=(m // tm,),
        in_specs=[pl.BlockSpec((tm, kmix), lambda i: (i, 0)),
                  pl.BlockSpec((tm, kmem), lambda i: (i, 0)),
                  pl.BlockSpec((None, kmix, d), lambda i: (layer, 0, 0), pipeline_mode=pl.Buffered(1)),
                  pl.BlockSpec((None, kmem, d), lambda i: (layer, kmix // kmem, 0), pipeline_mode=pl.Buffered(1)),
                  pl.BlockSpec((tm, d), lambda i: (i, 0)),
                  pl.BlockSpec((None, 1, d), lambda i: (layer, 0, 0)),
                  pl.BlockSpec((None, 1, d), lambda i: (layer, 0, 0))],
        out_specs=pl.BlockSpec((tm, d), lambda i: (i, 0)),
        out_shape=jax.ShapeDtypeStruct((m, d), F32),
        compiler_params=_cparams(("parallel",)),
    )(ymix, ymem, w_out, w_out, x, g, b)


def _ffn_kernel(x_ref, w1_ref, w2_ref, g_ref, b_ref, o_ref, xb_ref, acc_ref, *, alpha):
    f = pl.program_id(1)

    @pl.when(f == 0)
    def _():
        xb_ref[...] = x_ref[...].astype(BF16)
        acc_ref[...] = jnp.zeros_like(acc_ref)

    h = jnp.dot(xb_ref[...], w1_ref[...], preferred_element_type=F32)
    h = jnp.square(jnp.maximum(h, 0.0)).astype(BF16)
    acc_ref[...] += jnp.dot(h, w2_ref[...], preferred_element_type=F32)

    @pl.when(f == pl.num_programs(1) - 1)
    def _():
        y = alpha * x_ref[...] + acc_ref[...]
        o_ref[...] = _layer_norm(y, g_ref[...], b_ref[...])


def _ffn(x, w1, w2, g, b, layer, alpha, tm_target=512, tf_target=1024):
    m, d = x.shape
    dff = w1.shape[2]
    tm = _tile(m, tm_target)
    tf = _tile(dff, tf_target)
    return pl.pallas_call(
        functools.partial(_ffn_kernel, alpha=alpha),
        grid=(m // tm, dff // tf),
        in_specs=[pl.BlockSpec((tm, d), lambda i, f: (i, 0)),
                  pl.BlockSpec((None, d, tf), lambda i, f: (layer, 0, f)),
                  pl.BlockSpec((None, tf, d), lambda i, f: (layer, f, 0)),
                  pl.BlockSpec((None, 1, d), lambda i, f: (layer, 0, 0)),
                  pl.BlockSpec((None, 1, d), lambda i, f: (layer, 0, 0))],
        out_specs=pl.BlockSpec((tm, d), lambda i, f: (i, 0)),
        out_shape=jax.ShapeDtypeStruct((m, d), F32),
        scratch_shapes=[pltpu.VMEM((tm, d), BF16), pltpu.VMEM((tm, d), F32)],
        compiler_params=_cparams(("parallel", "arbitrary")),
    )(x, w1, w2, g, b)


def _memattn_kernel(q_ref, k_ref, v_ref, o_ref):
    scale = HEAD_DIM ** -0.5
    for h in range(N_MEM_HEADS):
        sl = slice(h * HEAD_DIM, (h + 1) * HEAD_DIM)
        s = lax.dot_general(q_ref[0, :, sl], k_ref[0, :, sl], (((1,), (1,)), ((), ())),
                            preferred_element_type=F32) * scale
        m = jnp.max(s, axis=-1, keepdims=True)
        p = jnp.exp(s - m)
        l = jnp.sum(p, axis=-1, keepdims=True)
        o = jnp.dot(p.astype(BF16), v_ref[0, :, sl], preferred_element_type=F32)
        o_ref[0, :, sl] = (o / l).astype(o_ref.dtype)


def _memattn(proj, q_col_block, kv, tq_target=1024):
    bsz, seq, _ = proj.shape
    n_mem = kv.shape[1]
    w = N_MEM_HEADS * HEAD_DIM
    tq = _tile(seq, tq_target)
    return pl.pallas_call(
        _memattn_kernel,
        grid=(bsz, seq // tq),
        in_specs=[pl.BlockSpec((1, tq, w), lambda b, i: (b, i, q_col_block)),
                  pl.BlockSpec((1, n_mem, w), lambda b, i: (b, 0, 0)),
                  pl.BlockSpec((1, n_mem, w), lambda b, i: (b, 0, 1))],
        out_specs=pl.BlockSpec((1, tq, w), lambda b, i: (b, i, 0)),
        out_shape=jax.ShapeDtypeStruct((bsz, seq, w), BF16),
        compiler_params=_cparams(("parallel", "parallel")),
    )(proj, kv, kv)


def _rope(x, cos2, sin2):
    return x * cos2 + pltpu.roll(x, HEAD_DIM // 2, axis=1) * sin2


def _wattn_kernel(sink_ref, q_ref, kp_ref, kc_ref, kn_ref, vp_ref, vc_ref, vn_ref, cos_ref, sin_ref,
                  o_ref, qr_ref, kr_ref, vr_ref, *, tq, seq):
    i = pl.program_id(1)
    base = i * tq
    blk = WINDOW
    nqb = tq // blk
    scale = HEAD_DIM ** -0.5
    n_q_heads = N_KV_HEADS * GQA_GROUP

    cos_c = cos_ref[pl.ds(pl.multiple_of(base, blk), tq), :]
    sin_c = sin_ref[pl.ds(pl.multiple_of(base, blk), tq), :]
    p_start = pl.multiple_of(jnp.maximum(base - blk, 0), blk)
    n_start = pl.multiple_of(jnp.minimum(base + tq, seq - blk), blk)
    cos_p = cos_ref[pl.ds(p_start, blk), :]
    sin_p = sin_ref[pl.ds(p_start, blk), :]
    cos_n = cos_ref[pl.ds(n_start, blk), :]
    sin_n = sin_ref[pl.ds(n_start, blk), :]

    for h in range(n_q_heads):
        sl = slice(h * HEAD_DIM, (h + 1) * HEAD_DIM)
        qr_ref[:, sl] = (_rope(q_ref[0, :, sl].astype(F32), cos_c, sin_c) * scale).astype(BF16)
    for h in range(N_KV_HEADS):
        sl = slice(h * HEAD_DIM, (h + 1) * HEAD_DIM)
        kr_ref[0:blk, sl] = _rope(kp_ref[0, :, sl].astype(F32), cos_p, sin_p).astype(BF16)
        kr_ref[blk:blk + tq, sl] = _rope(kc_ref[0, :, sl].astype(F32), cos_c, sin_c).astype(BF16)
        kr_ref[blk + tq:, sl] = _rope(kn_ref[0, :, sl].astype(F32), cos_n, sin_n).astype(BF16)
    vr_ref[0:blk, :] = vp_ref[0]
    vr_ref[blk:blk + tq, :] = vc_ref[0]
    vr_ref[blk + tq:, :] = vn_ref[0]

    r = lax.broadcasted_iota(jnp.int32, (blk, 3 * blk), 0)
    c = lax.broadcasted_iota(jnp.int32, (blk, 3 * blk), 1)
    rel = c - blk - r
    band = jnp.abs(rel) <= WINDOW

    for jb in range(nqb):
        kpos = base + (jb - 1) * blk + c
        valid = band & (kpos >= 0) & (kpos < seq)
        bias = jnp.where(valid, 0.0, NEG_INF).astype(F32)
        for hk in range(N_KV_HEADS):
            ksl = slice(hk * HEAD_DIM, (hk + 1) * HEAD_DIM)
            kw = kr_ref[jb * blk:(jb + 3) * blk, ksl]
            vw = vr_ref[jb * blk:(jb + 3) * blk, ksl]
            qs = jnp.concatenate(
                [qr_ref[jb * blk:(jb + 1) * blk, (hk * GQA_GROUP + g) * HEAD_DIM:(hk * GQA_GROUP + g + 1) * HEAD_DIM]
                 for g in range(GQA_GROUP)], axis=0)
            s = lax.dot_general(qs, kw, (((1,), (1,)), ((), ())), preferred_element_type=F32)
            ps = []
            inv = []
            for g in range(GQA_GROUP):
                sk = sink_ref[hk * GQA_GROUP + g]
                sg = s[g * blk:(g + 1) * blk, :] + bias
                m = jnp.maximum(jnp.max(sg, axis=-1, keepdims=True), sk)
                p = jnp.exp(sg - m)
                den = jnp.sum(p, axis=-1, keepdims=True) + jnp.exp(sk - m)
                ps.append(p.astype(BF16))
                inv.append(1.0 / den)
            o = jnp.dot(jnp.concatenate(ps, axis=0), vw, preferred_element_type=F32)
            for g in range(GQA_GROUP):
                hq = hk * GQA_GROUP + g
                o_ref[0, jb * blk:(jb + 1) * blk, hq * HEAD_DIM:(hq + 1) * HEAD_DIM] = (
                    o[g * blk:(g + 1) * blk, :] * inv[g]).astype(o_ref.dtype)


def _wattn(proj, sink, cos2, sin2, tq_target=512):
    bsz, seq, _ = proj.shape
    blk = WINDOW
    qw = N_KV_HEADS * GQA_GROUP * HEAD_DIM
    kvw = N_KV_HEADS * HEAD_DIM
    tq = _tile(seq, tq_target)
    r = tq // blk
    nb = seq // blk
    kcol = qw // kvw
    vcol = kcol + 1
    in_specs = [
        pl.BlockSpec(memory_space=pltpu.SMEM),
        pl.BlockSpec((1, tq, qw), lambda b, i: (b, i, 0)),
        pl.BlockSpec((1, blk, kvw), lambda b, i: (b, jnp.maximum(i * r - 1, 0), kcol)),
        pl.BlockSpec((1, tq, kvw), lambda b, i: (b, i, kcol)),
        pl.BlockSpec((1, blk, kvw), lambda b, i: (b, jnp.minimum(i * r + r, nb - 1), kcol)),
        pl.BlockSpec((1, blk, kvw), lambda b, i: (b, jnp.maximum(i * r - 1, 0), vcol)),
        pl.BlockSpec((1, tq, kvw), lambda b, i: (b, i, vcol)),
        pl.BlockSpec((1, blk, kvw), lambda b, i: (b, jnp.minimum(i * r + r, nb - 1), vcol)),
        pl.BlockSpec((seq, HEAD_DIM), lambda b, i: (0, 0)),
        pl.BlockSpec((seq, HEAD_DIM), lambda b, i: (0, 0)),
    ]
    return pl.pallas_call(
        functools.partial(_wattn_kernel, tq=tq, seq=seq),
        grid=(bsz, seq // tq),
        in_specs=in_specs,
        out_specs=pl.BlockSpec((1, tq, qw), lambda b, i: (b, i, 0)),
        out_shape=jax.ShapeDtypeStruct((bsz, seq, qw), BF16),
        scratch_shapes=[pltpu.VMEM((tq, qw), BF16),
                        pltpu.VMEM((tq + 2 * blk, kvw), BF16),
                        pltpu.VMEM((tq + 2 * blk, kvw), BF16)],
        compiler_params=_cparams(("parallel", "parallel")),
    )(sink, proj, proj, proj, proj, proj, proj, proj, cos2, sin2)


def _cpow(e, re, im):
    mg = jnp.exp(e * re)
    return mg * jnp.cos(e * im), mg * jnp.sin(e * im)


def _pow_rows(hi, lo):
    (hi_re, hi_im), (lo_re, lo_im) = hi, lo
    rows_re, rows_im = [], []
    for a in range(hi_re.shape[0]):
        ar = hi_re[a:a + 1, :]
        ai = hi_im[a:a + 1, :]
        rows_re.append(ar * lo_re - ai * lo_im)
        rows_im.append(ar * lo_im + ai * lo_re)
    return jnp.concatenate(rows_re, axis=0), jnp.concatenate(rows_im, axis=0)


def _ssm_prep_kernel(rowp_ref, blre_ref, blim_ref, clre_ref, clim_ref, dcol_ref,
                     ktab_ref, win_ref, wout_ref, atab_ref):
    t_len = SSM_CHUNK
    half = SQ // 2
    lane = lax.broadcasted_iota(jnp.int32, (1, SQ), 1)
    is_re_l = (lane % half) < SSM_STATE
    is_f_l = lane < half

    lam_re = rowp_ref[0, 0, 0:1, :]
    lam_im = rowp_ref[0, 0, 1:2, :]
    dt = jnp.exp(rowp_ref[0, 0, 2:3, :])
    re = lam_re * dt
    im = lam_im * dt
    mag = jnp.exp(re)
    n_re = mag * jnp.cos(im) - 1.0
    n_im = mag * jnp.sin(im)
    den = lam_re * lam_re + lam_im * lam_im
    co_re = (n_re * lam_re + n_im * lam_im) / den
    co_im = (n_im * lam_re - n_re * lam_im) / den
    b_re = blre_ref[0, 0]
    b_im = blim_ref[0, 0]
    bb_re = co_re * b_re - co_im * b_im
    bb_im = co_re * b_im + co_im * b_re

    a_i = lax.broadcasted_iota(jnp.int32, (t_len // 8, SQ), 0)
    i_i = lax.broadcasted_iota(jnp.int32, (8, SQ), 0)
    up, dn = 8 * a_i, t_len - 8 - 8 * a_i
    hi_dn_up = _cpow(jnp.where(is_f_l, dn, up).astype(F32), re, im)
    hi_up_dn = _cpow(jnp.where(is_f_l, up, dn).astype(F32), re, im)
    l_re, l_im = _pow_rows(hi_dn_up, _cpow(jnp.where(is_f_l, 7 - i_i, i_i).astype(F32), re, im))
    p1 = jnp.where(is_re_l, bb_re, bb_im)
    p2 = jnp.where(is_re_l, -bb_im, bb_re)
    for c in range(SSM_GROUP):
        win_ref[0, 0, c * t_len:(c + 1) * t_len, :] = (
            l_re * p1[c:c + 1, :] + l_im * p2[c:c + 1, :]).astype(win_ref.dtype)

    mg_t = jnp.exp(t_len * re)
    a_re = mg_t * jnp.cos(t_len * im)
    a_im = mg_t * jnp.sin(t_len * im)
    atab_ref[0, 0, 0:1, :] = a_re
    atab_ref[0, 0, 1:2, :] = jnp.where(is_re_l, -a_im, a_im)
    atab_ref[0, 0, 2:8, :] = jnp.zeros((6, SQ), F32)

    c_re = clre_ref[0, 0]
    c_im = clim_ref[0, 0]
    blocks = []
    for ci in range(SSM_GROUP):
        br = bb_re[ci:ci + 1, :]
        bi = bb_im[ci:ci + 1, :]
        d_re = c_re * br - c_im * bi
        d_im = c_re * bi + c_im * br
        blocks.append(jnp.where(is_re_l, d_re, -d_im))
    dmat = jnp.concatenate(blocks, axis=0)

    t2_re, t2_im = _pow_rows(hi_up_dn, _cpow(jnp.where(is_f_l, i_i, 8 - i_i).astype(F32), re, im))
    k_row = lax.broadcasted_iota(jnp.int32, (t_len, SQ), 0)
    lt = jnp.where(is_re_l, t2_re, t2_im)
    lt = jnp.where(jnp.logical_and(jnp.logical_not(is_f_l), k_row == 0), 0.0, lt)

    def taps_of(lanes):
        return lax.dot_general(dmat[:, lanes], lt[:, lanes], (((1,), (1,)), ((), ())),
                               preferred_element_type=F32, precision=lax.Precision.HIGHEST)

    k_f = taps_of(slice(0, half))
    k_b = taps_of(slice(half, SQ))
    in_b_re = jnp.logical_and(lane >= half, lane < half + SSM_STATE)
    kb0 = jnp.sum(jnp.where(in_b_re, dmat, 0.0), axis=1, keepdims=True)
    tap0 = lax.broadcasted_iota(jnp.int32, (SSM_GROUP * SSM_GROUP, t_len), 1) == 0
    k_f = k_f + jnp.where(tap0, kb0 + dcol_ref[0, 0], 0.0)
    taps = jnp.concatenate([k_f, k_b], axis=1).astype(BF16).astype(F32)
    bits = pltpu.bitcast(taps, jnp.int32)
    ktab_ref[0, 0] = pltpu.roll(bits, 1, axis=1) | lax.shift_right_logical(bits, 16)

    t3_re, t3_im = _pow_rows(hi_up_dn, _cpow(jnp.where(is_f_l, i_i + 1, 8 - i_i).astype(F32), re, im))
    for c in range(SSM_GROUP):
        cr = c_re[c:c + 1, :]
        ci = c_im[c:c + 1, :]
        g_re = cr * t3_re - ci * t3_im
        g_im = cr * t3_im + ci * t3_re
        wout_ref[0, 0, c * t_len:(c + 1) * t_len, :] = jnp.where(is_re_l, g_re, -g_im).astype(wout_ref.dtype)


def _ssm_prep(lam_re, lam_im, log_dt, b_re, b_im, c_re, c_im, d_skip):
    ns, _, n_g, n_p = lam_re.shape
    cg = SSM_GROUP

    def lanes(a):
        return jnp.concatenate([a[:, 0], a[:, 0], a[:, 1], a[:, 1]], axis=-1)

    ldt = jnp.broadcast_to(log_dt[..., None], lam_re.shape)
    rowp = jnp.stack([lanes(lam_re), lanes(lam_im), lanes(ldt)] + [jnp.zeros((ns, n_g, SQ), F32)] * 5, axis=2)
    bl_re = lanes(jnp.swapaxes(b_re, -1, -2))
    bl_im = lanes(jnp.swapaxes(b_im, -1, -2))
    cl_re = lanes(c_re)
    cl_im = lanes(c_im)
    dcol =(jnp.eye(cg, dtype=F32)[None, None] * d_skip.reshape(ns, n_g, 1, cg)).reshape(ns, n_g, cg * cg, 1)

    def spec(*shape):
        nd = len(shape)
        return pl.BlockSpec((1, 1) + shape, lambda j, g: (j, g) + (0,) * nd)

    return pl.pallas_call(
        _ssm_prep_kernel,
        grid=(ns, n_g),
        in_specs=[spec(8, SQ), spec(cg, SQ), spec(cg, SQ), spec(cg, SQ), spec(cg, SQ),
                  spec(cg * cg, 1)],
        out_specs=[spec(cg * cg, 2 * SSM_CHUNK), spec(GW, SQ), spec(GW, SQ), spec(8, SQ)],
        out_shape=[jax.ShapeDtypeStruct((ns, n_g, cg * cg, 2 * SSM_CHUNK), jnp.int32),
                   jax.ShapeDtypeStruct((ns, n_g, GW, SQ), BF16),
                   jax.ShapeDtypeStruct((ns, n_g, GW, SQ), BF16),
                   jax.ShapeDtypeStruct((ns, n_g, 8, SQ), F32)],
        compiler_params=_cparams(("parallel", "parallel")),
    )(rowp, bl_re, bl_im, cl_re, cl_im, dcol)


def _ssm_state_kernel(*refs, n_in):
    win_ref = refs[n_in]
    for x_ref, s_ref in zip(refs[:n_in], refs[n_in + 1:]):
        s_ref[0] = jnp.dot(x_ref[0].astype(BF16), win_ref[0], preferred_element_type=F32)


def _ssm_state(xgs, win, layer):
    n_g = win.shape[1]
    n_in = len(xgs)
    return pl.pallas_call(
        functools.partial(_ssm_state_kernel, n_in=n_in),
        grid=(n_g,),
        in_specs=[pl.BlockSpec((1, xg.shape[1], GW), lambda g: (g, 0, 0)) for xg in xgs]
        + [pl.BlockSpec((None, 1, GW, SQ), lambda g: (layer, g, 0, 0))],
        out_specs=[pl.BlockSpec((1, xg.shape[1], SQ), lambda g: (g, 0, 0)) for xg in xgs],
        out_shape=[jax.ShapeDtypeStruct((n_g, xg.shape[1], SQ), F32) for xg in xgs],
        compiler_params=_cparams(("parallel",)),
    )(*xgs, win)


SCAN_GROUPS = 8


def _ssm_scan_kernel(s_ref, a_ref, h_ref, *, n_chunks):
    half = SQ // 2
    nb = s_ref.shape[2]
    for k in range(s_ref.shape[0]):
        a1_f = a_ref[k, 0:1, 0:half]
        a2_f = a_ref[k, 1:2, 0:half]
        a1_b = a_ref[k, 0:1, half:]
        a2_b = a_ref[k, 1:2, half:]
        hf = jnp.zeros((nb, half), F32)
        hb = jnp.zeros((nb, half), F32)
        for i in range(n_chunks):
            j = n_chunks - 1 - i
            h_ref[k, i, :, 0:half] = hf.astype(h_ref.dtype)
            h_ref[k, j, :, half:] = hb.astype(h_ref.dtype)
            hf = a1_f * hf + a2_f * pltpu.roll(hf, SSM_STATE, axis=1) + s_ref[k, i, :, 0:half]
            hb = a1_b * hb + a2_b * pltpu.roll(hb, SSM_STATE, axis=1) + s_ref[k, j, :, half:]


def _ssm_scan(s, atab, layer):
    n_g, n_chunks, nb, _ = s.shape
    gb = SCAN_GROUPS if n_g % SCAN_GROUPS == 0 else 1
    return pl.pallas_call(
        functools.partial(_ssm_scan_kernel, n_chunks=n_chunks),
        grid=(n_g // gb,),
        in_specs=[pl.BlockSpec((gb, n_chunks, nb, SQ), lambda g: (g, 0, 0, 0)),
                  pl.BlockSpec((None, gb, 8, SQ), lambda g: (layer, g, 0, 0))],
        out_specs=pl.BlockSpec((gb, n_chunks, nb, SQ), lambda g: (g, 0, 0, 0)),
        out_shape=jax.ShapeDtypeStruct((n_g, n_chunks, nb, SQ), BF16),
        compiler_params=_cparams(("parallel",)),
    )(s, atab)


def _gelu_tanh(y):
    return 0.5 * y * (1.0 + jnp.tanh(math.sqrt(2.0 / math.pi) * (y + 0.044715 * (y * y * y))))


def _ssm_out_kernel(*refs, n_in):
    x_refs = refs[:n_in]
    h_refs = refs[n_in:2 * n_in]
    ktab_ref, wout_ref = refs[2 * n_in:2 * n_in + 2]
    z_refs = refs[2 * n_in + 2:3 * n_in + 2]
    m_ref = refs[3 * n_in + 2]
    t_len = SSM_CHUNK

    def build(ci, carry):
        r0 = pl.multiple_of(ci * t_len, t_len)
        for co in range(SSM_GROUP):
            v = ktab_ref[0, pl.ds(ci * SSM_GROUP + co, 1), :]
            p8 = jnp.broadcast_to(v, (8, 2 * t_len))
            for a in range(t_len // 16):
                w = pltpu.roll(p8, 16 * a, axis=1, stride=2, stride_axis=0)[:, :t_len]
                m_ref[pl.ds(pl.multiple_of(r0 + 16 * a, 16), 16), co * t_len:(co + 1) * t_len] = (
                    pltpu.bitcast(w, m_ref.dtype))
        return carry

    lax.fori_loop(0, SSM_GROUP, build, 0)
    for x_ref, h_ref, z_ref in zip(x_refs, h_refs, z_refs):
        y = jnp.dot(x_ref[0].astype(BF16), m_ref[...], preferred_element_type=F32)
        y = y + lax.dot_general(h_ref[0], wout_ref[0], (((1,), (1,)), ((), ())), preferred_element_type=F32)
        z_ref[0] = _gelu_tanh(y).astype(z_ref.dtype)


def _ssm_out(xgs, hprevs, ktab, wout, layer):
    n_g = ktab.shape[1]
    n_in = len(xgs)
    return pl.pallas_call(
        functools.partial(_ssm_out_kernel, n_in=n_in),
        grid=(n_g,),
        in_specs=[pl.BlockSpec((1, xg.shape[1], GW), lambda g: (g, 0, 0)) for xg in xgs]
        + [pl.BlockSpec((1, h.shape[1], SQ), lambda g: (g, 0, 0)) for h in hprevs]
        + [pl.BlockSpec((None, 1, SSM_GROUP * SSM_GROUP, 2 * SSM_CHUNK), lambda g: (layer, g, 0, 0)),
           pl.BlockSpec((None, 1, GW, SQ), lambda g: (layer, g, 0, 0))],
        out_specs=[pl.BlockSpec((1, xg.shape[1], GW), lambda g: (g, 0, 0)) for xg in xgs],
        out_shape=[jax.ShapeDtypeStruct((n_g, xg.shape[1], GW), F32) for xg in xgs],
        scratch_shapes=[pltpu.VMEM((GW, GW), BF16)],
        compiler_params=_cparams(("parallel",)),
    )(*xgs, *hprevs, ktab, wout)


def _rope_tables(seq):
    inv_freq = ROPE_THETA ** (-jnp.arange(0, HEAD_DIM, 2, dtype=F32) / HEAD_DIM)
    ang = jnp.arange(seq, dtype=F32)[:, None] * inv_freq[None, :]
    cos, sin = jnp.cos(ang), jnp.sin(ang)
    return jnp.concatenate([cos, cos], axis=-1), jnp.concatenate([-sin, sin], axis=-1)


def kernel(x_prompt, x_sample, mem_prompt, mem_sample, ssm_w_in, ssm_lam_re, ssm_lam_im, ssm_log_dt, ssm_b_re,
           ssm_b_im, ssm_c_re, ssm_c_im, ssm_d, ssm_w_glu, attn_w_in, attn_sink, w_mem_kv, w_out, ln1_g, ln1_b,
           w_ff1, w_ff2, ln2_g, ln2_b):
    depth = w_out.shape[0]
    d_model = x_prompt.shape[-1]
    alpha = (2 * depth) ** 0.25
    mix_w = ssm_d.shape[-1]
    n_groups = mix_w // SSM_GROUP
    mem_w = N_MEM_HEADS * HEAD_DIM

    ssm_w_in_b = ssm_w_in.astype(BF16)
    ssm_w_glu_b = ssm_w_glu.astype(BF16)
    attn_w_in_b = attn_w_in.astype(BF16)
    w_mem_kv_b = w_mem_kv.astype(BF16)
    w_out_b = w_out.astype(BF16)
    w_ff1_b = w_ff1.astype(BF16)
    w_ff2_b = w_ff2.astype(BF16)

    ktab, win, wout, atab = _ssm_prep(ssm_lam_re, ssm_lam_im, ssm_log_dt, ssm_b_re, ssm_b_im,
                                      ssm_c_re, ssm_c_im, ssm_d.reshape(-1, n_groups, SSM_GROUP))
    shapes = [x_prompt.shape, x_sample.shape]
    ropes = [_rope_tables(s[1]) for s in shapes]
    xs = [x_prompt.reshape(-1, d_model), x_sample.reshape(-1, d_model)]
    mems = [mem_prompt, mem_sample]

    ln1_g3, ln1_b3, ln2_g3, ln2_b3 = [a.reshape(depth, 1, d_model) for a in (ln1_g, ln1_b, ln2_g, ln2_b)]

    for i in range(depth):
        j = i // 2
        kvs = [_rowmm(m.reshape(-1, d_model), w_mem_kv_b, i).reshape(m.shape[0], m.shape[1], -1) for m in mems]
        if i % 2 == 0:
            xgs, qms = zip(*[_ssm_inproj(x.reshape(s), ssm_w_in_b, j, mix_w) for x, s in zip(xs, shapes)])
            ss = _ssm_state(xgs, win, j)
            hprevs = [_ssm_scan(s.reshape(n_groups, -1, shp[0], SQ), atab, j).reshape(n_groups, -1, SQ)
                      for s, shp in zip(ss, shapes)]
            zs = _ssm_out(xgs, hprevs, ktab, wout, j)
            ymix = [_glu(z, ssm_w_glu_b, j, shp[0]) for z, shp in zip(zs, shapes)]
            ymem = [_memattn(q, 0, kv) for q, kv in zip(qms, kvs)]
        else:
            projs = [_rowmm(x, attn_w_in_b, j).reshape(s[0], s[1], -1) for x, s in zip(xs, shapes)]
            ymix = [_wattn(p, attn_sink[j], cos2, sin2) for p, (cos2, sin2) in zip(projs, ropes)]
            qcol = (mix_w + 2 * N_KV_HEADS * HEAD_DIM) // mem_w
            ymem = [_memattn(p, qcol, kv) for p, kv in zip(projs, kvs)]
        xs = [_oproj(ym.reshape(-1, mix_w), ye.reshape(-1, mem_w), w_out_b, x, ln1_g3, ln1_b3, i, alpha)
              for ym, ye, x in zip(ymix, ymem, xs)]
        xs = [_ffn(x, w_ff1_b, w_ff2_b, ln2_g3, ln2_b3, i, alpha) for x in xs]

    return (xs[0].reshape(shapes[0]), xs[1].reshape(shapes[1]))
```

```python
import functools
import math

import jax
import jax.numpy as jnp
from jax import lax
from jax.experimental import pallas as pl
from jax.experimental.pallas import tpu as pltpu

F32 = jnp.float32
BF16 = jnp.bfloat16

HEAD_DIM = 128
N_KV_HEADS = 4
GQA_GROUP = 3
WINDOW = 128
N_MEM_HEADS = 4
SSM_GROUP = 16
SSM_STATE = 64
SSM_CHUNK = 128
ROPE_THETA = 10000.0
LN_EPS = 1e-5
NEG_INF = -1e30
SQ = 2 * SSM_STATE * 2
GW = SSM_GROUP * SSM_CHUNK
V7X_VMEM_LIMIT_MB = 56


def _cparams(sem, vmem_mb=V7X_VMEM_LIMIT_MB):
    return pltpu.CompilerParams(dimension_semantics=sem, vmem_limit_bytes=vmem_mb * 1024 * 1024)


def _tile(n, target):
    if n <= target:
        return n
    for t in range(target, 7, -1):
        if n % t == 0 and t % 8 == 0:
            return t
    return n


def _layer_norm(y, g, b):
    mu = jnp.mean(y, axis=-1, keepdims=True)
    d = y - mu
    var = jnp.mean(d * d, axis=-1, keepdims=True)
    return d * lax.rsqrt(var + LN_EPS) * g + b


def _rowmm_kernel(x_ref, w_ref, o_ref, *, n_chunk):
    xb = x_ref[...].astype(BF16)
    n = w_ref.shape[1]
    for n0 in range(0, n, n_chunk):
        o_ref[:, n0:n0 + n_chunk] = jnp.dot(
            xb, w_ref[:, n0:n0 + n_chunk], preferred_element_type=F32).astype(o_ref.dtype)


def _rowmm(x, w, layer, tm_target=512):
    m, k = x.shape
    n = w.shape[2]
    tm = _tile(m, tm_target)
    n_chunk = 512 if n % 512 == 0 else n
    return pl.pallas_call(
        functools.partial(_rowmm_kernel, n_chunk=n_chunk),
        grid=(m // tm,),
        in_specs=[pl.BlockSpec((tm, k), lambda i: (i, 0)),
                  pl.BlockSpec((None, k, n), lambda i: (layer, 0, 0), pipeline_mode=pl.Buffered(1))],
        out_specs=pl.BlockSpec((tm, n), lambda i: (i, 0)),
        out_shape=jax.ShapeDtypeStruct((m, n), BF16),
        compiler_params=_cparams(("parallel",)),
    )(x, w)


RELAYOUT_ROWS = 8
CH_PASS = 512
SLAB_PITCH = CH_PASS + 8


def _chunks_per_step(bsz):
    assert RELAYOUT_ROWS % bsz == 0, "batch must divide the 8-row relayout tile"
    return RELAYOUT_ROWS // bsz


def _ssm_inproj_kernel(x_ref, wu_ref, wq_ref, xg_ref, q_ref, tt_ref, *, bsz, cpb):
    d = x_ref.shape[-1]
    xb = x_ref[...].reshape(RELAYOUT_ROWS * SSM_CHUNK, d).astype(BF16)
    q = jnp.dot(xb, wq_ref[...], preferred_element_type=F32)
    q_ref[...] = q.reshape(q_ref.shape).astype(q_ref.dtype)
    for n in range(wu_ref.shape[1] // CH_PASS):
        u = jnp.dot(xb, wu_ref[:, n * CH_PASS:(n + 1) * CH_PASS], preferred_element_type=F32)
        for b in range(bsz):
            for cc in range(cpb):
                src = (b * cpb + cc) * SSM_CHUNK
                slot = cc * bsz + b
                tt_ref[slot * SLAB_PITCH:slot * SLAB_PITCH + CH_PASS, :] = u[src:src + SSM_CHUNK, :].T
        for r in range(CH_PASS):
            g = (n * CH_PASS + r) // SSM_GROUP
            c = r % SSM_GROUP
            xg_ref[g, :, c * SSM_CHUNK:(c + 1) * SSM_CHUNK] = tt_ref[pl.ds(r, RELAYOUT_ROWS, stride=SLAB_PITCH), :]


def _ssm_inproj(x, w_in, layer, mix_w):
    bsz, seq, d = x.shape
    nc = seq // SSM_CHUNK
    cpb = _chunks_per_step(bsz)
    assert nc % cpb == 0 and mix_w % CH_PASS == 0
    n_g = mix_w // SSM_GROUP
    mem_w = w_in.shape[2] - mix_w
    x4 = x.reshape(bsz, nc, SSM_CHUNK, d)
    return pl.pallas_call(
        functools.partial(_ssm_inproj_kernel, bsz=bsz, cpb=cpb),
        grid=(nc // cpb,),
        in_specs=[pl.BlockSpec((bsz, cpb, SSM_CHUNK, d), lambda i: (0, i, 0, 0)),
                  pl.BlockSpec((None, d, mix_w), lambda i: (layer, 0, 0), pipeline_mode=pl.Buffered(1)),
                  pl.BlockSpec((None, d, mem_w), lambda i: (layer, 0, mix_w // mem_w),
                               pipeline_mode=pl.Buffered(1))],
        out_specs=[pl.BlockSpec((n_g, RELAYOUT_ROWS, GW), lambda i: (0, i, 0)),
                   pl.BlockSpec((bsz, cpb * SSM_CHUNK, mem_w), lambda i: (0, i, 0))],
        out_shape=[jax.ShapeDtypeStruct((n_g, nc * bsz, GW), F32),
                   jax.ShapeDtypeStruct((bsz, seq, mem_w), BF16)],
        scratch_shapes=[pltpu.VMEM((RELAYOUT_ROWS * SLAB_PITCH, SSM_CHUNK), F32)],
        compiler_params=_cparams(("parallel",)),
    )(x4, w_in, w_in)


def _glu_kernel(z_ref, wa_ref, wg_ref, o_ref, tt_ref, zt_ref, *, bsz, cpb):
    n_out = wa_ref.shape[1]
    for n in range(zt_ref.shape[1] // CH_PASS):
        for r in range(CH_PASS):
            g = (n * CH_PASS + r) // SSM_GROUP
            c = r % SSM_GROUP
            tt_ref[pl.ds(r, RELAYOUT_ROWS, stride=SLAB_PITCH), :] = z_ref[g, :, c * SSM_CHUNK:(c + 1) * SSM_CHUNK]
        for b in range(bsz):
            for cc in range(cpb):
                dst = (b * cpb + cc) * SSM_CHUNK
                slot = cc * bsz + b
                zt_ref[dst:dst + SSM_CHUNK, n * CH_PASS:(n + 1) * CH_PASS] = (
                    tt_ref[slot * SLAB_PITCH:slot * SLAB_PITCH + CH_PASS, :].T.astype(zt_ref.dtype))
    z = zt_ref[...]
    for n0 in range(0, n_out, CH_PASS):
        a = jnp.dot(z, wa_ref[:, n0:n0 + CH_PASS], preferred_element_type=F32)
        g = jnp.dot(z, wg_ref[:, n0:n0 + CH_PASS], preferred_element_type=F32)
        y = (a * jax.nn.sigmoid(g)).astype(o_ref.dtype)
        o_ref[:, :, n0:n0 + CH_PASS] = y.reshape(bsz, cpb * SSM_CHUNK, CH_PASS)


def _glu(z, w_glu, layer, bsz):
    n_g, rows, _ = z.shape
    mix_w = n_g * SSM_GROUP
    cpb = _chunks_per_step(bsz)
    nc = rows // bsz
    return pl.pallas_call(
        functools.partial(_glu_kernel, bsz=bsz, cpb=cpb),
        grid=(nc // cpb,),
        in_specs=[pl.BlockSpec((n_g, RELAYOUT_ROWS, GW), lambda i: (0, i, 0)),
                  pl.BlockSpec((None, mix_w, mix_w), lambda i: (layer, 0, 0), pipeline_mode=pl.Buffered(1)),
                  pl.BlockSpec((None, mix_w, mix_w), lambda i: (layer, 0, 1), pipeline_mode=pl.Buffered(1))],
        out_specs=pl.BlockSpec((bsz, cpb * SSM_CHUNK, mix_w), lambda i: (0, i, 0)),
        out_shape=jax.ShapeDtypeStruct((bsz, nc * SSM_CHUNK, mix_w), BF16),
        scratch_shapes=[pltpu.VMEM((RELAYOUT_ROWS * SLAB_PITCH, SSM_CHUNK), F32),
                        pltpu.VMEM((RELAYOUT_ROWS * SSM_CHUNK, mix_w), BF16)],
        compiler_params=_cparams(("parallel",)),
    )(z, w_glu, w_glu)


def _oproj_kernel(ymix_ref, ymem_ref, wa_ref, wb_ref, x_ref, g_ref, b_ref, o_ref, *, alpha, n_sub):
    sub = x_ref.shape[0] // n_sub
    for r in range(n_sub):
        rs = slice(r * sub, (r + 1) * sub)
        acc = jnp.dot(ymix_ref[rs, :], wa_ref[...], preferred_element_type=F32)
        acc = acc + jnp.dot(ymem_ref[rs, :], wb_ref[...], preferred_element_type=F32)
        y = alpha * x_ref[rs, :] + acc
        o_ref[rs, :] = _layer_norm(y, g_ref[...], b_ref[...])


def _oproj(ymix, ymem, w_out, x, g, b, layer, alpha, tm_target=1024, n_sub=4):
    m, d = x.shape
    kmix = ymix.shape[1]
    kmem = ymem.shape[1]
    tm = _tile(m, tm_target)
    if tm % (8 * n_sub):
        n_sub = 1
    return pl.pallas_call(
        functools.partial(_oproj_kernel, alpha=alpha, n_sub=n_sub),
        grid=(m // tm,),
        in_specs=[pl.BlockSpec((tm, kmix), lambda i: (i, 0)),
                  pl.BlockSpec((tm, kmem), lambda i: (i, 0)),
                  pl.BlockSpec((None, kmix, d), lambda i: (layer, 0, 0), pipeline_mode=pl.Buffered(1)),
                  pl.BlockSpec((None, kmem, d), lambda i: (layer, kmix // kmem, 0), pipeline_mode=pl.Buffered(1)),
                  pl.BlockSpec((tm, d), lambda i: (i, 0)),
                  pl.BlockSpec((None, 1, d), lambda i: (layer, 0, 0)),
                  pl.BlockSpec((None, 1, d), lambda i: (layer, 0, 0))],
        out_specs=pl.BlockSpec((tm, d), lambda i: (i, 0)),
        out_shape=jax.ShapeDtypeStruct((m, d), F32),
        compiler_params=_cparams(("parallel",)),
    )(ymix, ymem, w_out, w_out, x, g, b)


def _ffn_kernel(x_ref, w1_ref, w2_ref, g_ref, b_ref, o_ref, xb_ref, acc_ref, *, alpha):
    f = pl.program_id(1)

    @pl.when(f == 0)
    def _():
        xb_ref[...] = x_ref[...].astype(BF16)
        acc_ref[...] = jnp.zeros_like(acc_ref)

    h = jnp.dot(xb_ref[...], w1_ref[...], preferred_element_type=F32)
    h = jnp.square(jnp.maximum(h, 0.0)).astype(BF16)
    acc_ref[...] += jnp.dot(h, w2_ref[...], preferred_element_type=F32)

    @pl.when(f == pl.num_programs(1) - 1)
    def _():
        y = alpha * x_ref[...] + acc_ref[...]
        o_ref[...] = _layer_norm(y, g_ref[...], b_ref[...])


def _ffn(x, w1, w2, g, b, layer, alpha, tm_target=512, tf_target=1024):
    m, d = x.shape
    dff = w1.shape[2]
    tm = _tile(m, tm_target)
    tf = _tile(dff, tf_target)
    return pl.pallas_call(
        functools.partial(_ffn_kernel, alpha=alpha),
        grid=(m // tm, dff // tf),
        in_specs=[pl.BlockSpec((tm, d), lambda i, f: (i, 0)),
                  pl.BlockSpec((None, d, tf), lambda i, f: (layer, 0, f)),
                  pl.BlockSpec((None, tf, d), lambda i, f: (layer, f, 0)),
                  pl.BlockSpec((None, 1, d), lambda i, f: (layer, 0, 0)),
                  pl.BlockSpec((None, 1, d), lambda i, f: (layer, 0, 0))],
        out_specs=pl.BlockSpec((tm, d), lambda i, f: (i, 0)),
        out_shape=jax.ShapeDtypeStruct((m, d), F32),
        scratch_shapes=[pltpu.VMEM((tm, d), BF16), pltpu.VMEM((tm, d), F32)],
        compiler_params=_cparams(("parallel", "arbitrary")),
    )(x, w1, w2, g, b)


def _memattn_kernel(q_ref, k_ref, v_ref, o_ref):
    scale = HEAD_DIM ** -0.5
    for h in range(N_MEM_HEADS):
        sl = slice(h * HEAD_DIM, (h + 1) * HEAD_DIM)
        s = lax.dot_general(q_ref[0, :, sl], k_ref[0, :, sl], (((1,), (1,)), ((), ())),
                            preferred_element_type=F32) * scale
        m = jnp.max(s, axis=-1, keepdims=True)
        p = jnp.exp(s - m)
        l = jnp.sum(p, axis=-1, keepdims=True)
        o = jnp.dot(p.astype(BF16), v_ref[0, :, sl], preferred_element_type=F32)
        o_ref[0, :, sl] = (o / l).astype(o_ref.dtype)


def _memattn(proj, q_col_block, kv, tq_target=1024):
    bsz, seq, _ = proj.shape
    n_mem = kv.shape[1]
    w = N_MEM_HEADS * HEAD_DIM
    tq = _tile(seq, tq_target)
    return pl.pallas_call(
        _memattn_kernel,
        grid=(bsz, seq // tq),
        in_specs=[pl.BlockSpec((1, tq, w), lambda b, i: (b, i, q_col_block)),
                  pl.BlockSpec((1, n_mem, w), lambda b, i: (b, 0, 0)),
                  pl.BlockSpec((1, n_mem, w), lambda b, i: (b, 0, 1))],
        out_specs=pl.BlockSpec((1, tq, w), lambda b, i: (b, i, 0)),
        out_shape=jax.ShapeDtypeStruct((bsz, seq, w), BF16),
        compiler_params=_cparams(("parallel", "parallel")),
    )(proj, kv, kv)


def _rope(x, cos2, sin2):
    return x * cos2 + pltpu.roll(x, HEAD_DIM // 2, axis=1) * sin2


def _wattn_kernel(sink_ref, q_ref, kp_ref, kc_ref, kn_ref, vp_ref, vc_ref, vn_ref,
                  o_ref, kr_ref, vr_ref, *, tq, seq):
    i = pl.program_id(1)
    base = i * tq
    blk = WINDOW
    nqb = tq // blk

    kr_ref[0:blk, :] = kp_ref[0]
    kr_ref[blk:blk + tq, :] = kc_ref[0]
    kr_ref[blk + tq:, :] = kn_ref[0]
    ones = jnp.ones((tq + 2 * blk, HEAD_DIM), BF16)
    for h in range(N_KV_HEADS):
        src = slice(h * HEAD_DIM, (h + 1) * HEAD_DIM)
        dst = slice(2 * h * HEAD_DIM, (2 * h + 1) * HEAD_DIM)
        vr_ref[0:blk, dst] = vp_ref[0, :, src]
        vr_ref[blk:blk + tq, dst] = vc_ref[0, :, src]
        vr_ref[blk + tq:, dst] = vn_ref[0, :, src]
        vr_ref[:, (2 * h + 1) * HEAD_DIM:(2 * h + 2) * HEAD_DIM] = ones

    r = lax.broadcasted_iota(jnp.int32, (blk, 3 * blk), 0)
    c = lax.broadcasted_iota(jnp.int32, (blk, 3 * blk), 1)
    rel = c - blk - r
    band = jnp.abs(rel) <= WINDOW

    for jb in range(nqb):
        kpos = base + (jb - 1) * blk + c
        valid = band & (kpos >= 0) & (kpos < seq)
        bias = jnp.where(valid, 0.0, NEG_INF).astype(F32)
        for hk in range(N_KV_HEADS):
            ksl = slice(hk * HEAD_DIM, (hk + 1) * HEAD_DIM)
            kw = kr_ref[jb * blk:(jb + 3) * blk, ksl]
            vw = vr_ref[jb * blk:(jb + 3) * blk, 2 * hk * HEAD_DIM:(2 * hk + 2) * HEAD_DIM]
            qs = jnp.concatenate(
                [q_ref[0, jb * blk:(jb + 1) * blk, (hk * GQA_GROUP + g) * HEAD_DIM:(hk * GQA_GROUP + g + 1) * HEAD_DIM]
                 for g in range(GQA_GROUP)], axis=0)
            s = lax.dot_general(qs, kw, (((1,), (1,)), ((), ())), preferred_element_type=F32)
            ps = []
            sink_terms = []
            for g in range(GQA_GROUP):
                sk = sink_ref[hk * GQA_GROUP + g]
                sg = s[g * blk:(g + 1) * blk, :] + bias
                m = jnp.maximum(jnp.max(sg, axis=-1, keepdims=True), sk)
                ps.append(jnp.exp(sg - m).astype(BF16))
                sink_terms.append(jnp.exp(sk - m))
            pcat = jnp.concatenate(ps, axis=0)
            o = jnp.dot(pcat, vw, preferred_element_type=F32)
            for g in range(GQA_GROUP):
                hq = hk * GQA_GROUP + g
                den = o[g * blk:(g + 1) * blk, HEAD_DIM:] + sink_terms[g]
                o_ref[0, jb * blk:(jb + 1) * blk, hq * HEAD_DIM:(hq + 1) * HEAD_DIM] = (
                    o[g * blk:(g + 1) * blk, :HEAD_DIM] / den).astype(o_ref.dtype)


def _attn_inproj_kernel(x_ref, w_ref, cos_ref, sin_ref, o_ref, *, n_q_heads, n_rope_heads, n_chunk):
    xb = x_ref[...].astype(BF16)
    cos2 = cos_ref[...]
    sin2 = sin_ref[...]
    scale = HEAD_DIM ** -0.5
    for n0 in range(0, w_ref.shape[1], n_chunk):
        y = jnp.dot(xb, w_ref[:, n0:n0 + n_chunk], preferred_element_type=F32)
        for c0 in range(0, n_chunk, HEAD_DIM):
            head = (n0 + c0) // HEAD_DIM
            blk = y[:, c0:c0 + HEAD_DIM]
            if head < n_rope_heads:
                blk = _rope(blk, cos2, sin2)
            if head < n_q_heads:
                blk = blk * scale
            o_ref[:, n0 + c0:n0 + c0 + HEAD_DIM] = blk.astype(o_ref.dtype)


def _attn_inproj(x, w, layer, cos2, sin2, seq, tm_target=512):
    m, k = x.shape
    n = w.shape[2]
    tm = _tile(seq, tm_target)
    n_q_heads = N_KV_HEADS * GQA_GROUP
    return pl.pallas_call(
        functools.partial(_attn_inproj_kernel, n_q_heads=n_q_heads, n_rope_heads=n_q_heads + N_KV_HEADS,
                          n_chunk=4 * HEAD_DIM),
        grid=(m // tm,),
        in_specs=[pl.BlockSpec((tm, k), lambda i: (i, 0)),
                  pl.BlockSpec((None, k, n), lambda i: (layer, 0, 0), pipeline_mode=pl.Buffered(1)),
                  pl.BlockSpec((tm, HEAD_DIM), lambda i: (i % (seq // tm), 0)),
                  pl.BlockSpec((tm, HEAD_DIM), lambda i: (i % (seq // tm), 0))],
        out_specs=pl.BlockSpec((tm, n), lambda i: (i, 0)),
        out_shape=jax.ShapeDtypeStruct((m, n), BF16),
        compiler_params=_cparams(("parallel",)),
    )(x, w, cos2, sin2)


def _wattn(proj, sink, tq_target=512):
    bsz, seq, _ = proj.shape
    blk = WINDOW
    qw = N_KV_HEADS * GQA_GROUP * HEAD_DIM
    kvw = N_KV_HEADS * HEAD_DIM
    tq = _tile(seq, tq_target)
    r = tq // blk
    nb = seq // blk
    kcol = qw // kvw
    vcol = kcol + 1
    in_specs = [
        pl.BlockSpec(memory_space=pltpu.SMEM),
        pl.BlockSpec((1, tq, qw), lambda b, i: (b, i, 0)),
        pl.BlockSpec((1, blk, kvw), lambda b, i: (b, jnp.maximum(i * r - 1, 0), kcol)),
        pl.BlockSpec((1, tq, kvw), lambda b, i: (b, i, kcol)),
        pl.BlockSpec((1, blk, kvw), lambda b, i: (b, jnp.minimum(i * r + r, nb - 1), kcol)),
        pl.BlockSpec((1, blk, kvw), lambda b, i: (b, jnp.maximum(i * r - 1, 0), vcol)),
        pl.BlockSpec((1, tq, kvw), lambda b, i: (b, i, vcol)),
        pl.BlockSpec((1, blk, kvw), lambda b, i: (b, jnp.minimum(i * r + r, nb - 1), vcol)),
    ]
    return pl.pallas_call(
        functools.partial(_wattn_kernel, tq=tq, seq=seq),
        grid=(bsz, seq // tq),
        in_specs=in_specs,
        out_specs=pl.BlockSpec((1, tq, qw), lambda b, i: (b, i, 0)),
        out_shape=jax.ShapeDtypeStruct((bsz, seq, qw), BF16),
        scratch_shapes=[pltpu.VMEM((tq + 2 * blk, kvw), BF16),
                        pltpu.VMEM((tq + 2 * blk, 2 * kvw), BF16)],
        compiler_params=_cparams(("parallel", "parallel")),
    )(sink, proj, proj, proj, proj, proj, proj, proj)


def _cpow(e, re, im):
    mg = jnp.exp(e * re)
    return mg * jnp.cos(e * im), mg * jnp.sin(e * im)


def _pow_rows(hi, lo):
    (hi_re, hi_im), (lo_re, lo_im) = hi, lo
    rows_re, rows_im = [], []
    for a in range(hi_re.shape[0]):
        ar = hi_re[a:a + 1, :]
        ai = hi_im[a:a + 1, :]
        rows_re.append(ar * lo_re - ai * lo_im)
        rows_im.append(ar * lo_im + ai * lo_re)
    return jnp.concatenate(rows_re, axis=0), jnp.concatenate(rows_im, axis=0)


def _ssm_prep_kernel(rowp_ref, blre_ref, blim_ref, clre_ref, clim_ref, dcol_ref,
                     ktab_ref, win_ref, wout_ref, atab_ref):
    t_len = SSM_CHUNK
    half = SQ // 2
    lane = lax.broadcasted_iota(jnp.int32, (1, SQ), 1)
    is_re_l = (lane % half) < SSM_STATE
    is_f_l = lane < half

    lam_re = rowp_ref[0, 0, 0:1, :]
    lam_im = rowp_ref[0, 0, 1:2, :]
    dt = jnp.exp(rowp_ref[0, 0, 2:3, :])
    re = lam_re * dt
    im = lam_im * dt
    mag = jnp.exp(re)
    n_re = mag * jnp.cos(im) - 1.0
    n_im = mag * jnp.sin(im)
    den = lam_re * lam_re + lam_im * lam_im
    co_re = (n_re * lam_re + n_im * lam_im) / den
    co_im = (n_im * lam_re - n_re * lam_im) / den
    b_re = blre_ref[0, 0]
    b_im = blim_ref[0, 0]
    bb_re = co_re * b_re - co_im * b_im
    bb_im = co_re * b_im + co_im * b_re

    a_i = lax.broadcasted_iota(jnp.int32, (t_len // 8, SQ), 0)
    i_i = lax.broadcasted_iota(jnp.int32, (8, SQ), 0)
    up, dn = 8 * a_i, t_len - 8 - 8 * a_i
    hi_dn_up = _cpow(jnp.where(is_f_l, dn, up).astype(F32), re, im)
    hi_up_dn = _cpow(jnp.where(is_f_l, up, dn).astype(F32), re, im)
    l_re, l_im = _pow_rows(hi_dn_up, _cpow(jnp.where(is_f_l, 7 - i_i, i_i).astype(F32), re, im))
    p1 = jnp.where(is_re_l, bb_re, bb_im)
    p2 = jnp.where(is_re_l, -bb_im, bb_re)
    for c in range(SSM_GROUP):
        win_ref[0, 0, c * t_len:(c + 1) * t_len, :] = (
            l_re * p1[c:c + 1, :] + l_im * p2[c:c + 1, :]).astype(win_ref.dtype)

    mg_t = jnp.exp(t_len * re)
    a_re = mg_t * jnp.cos(t_len * im)
    a_im = mg_t * jnp.sin(t_len * im)
    atab_ref[0, 0, 0:1, :] = a_re
    atab_ref[0, 0, 1:2, :] = jnp.where(is_re_l, -a_im, a_im)
    atab_ref[0, 0, 2:8, :] = jnp.zeros((6, SQ), F32)

    c_re = clre_ref[0, 0]
    c_im = clim_ref[0, 0]
    blocks = []
    for ci in range(SSM_GROUP):
        br = bb_re[ci:ci + 1, :]
        bi = bb_im[ci:ci + 1, :]
        d_re = c_re * br - c_im * bi
        d_im = c_re * bi + c_im * br
        blocks.append(jnp.where(is_re_l, d_re, -d_im))
    dmat = jnp.concatenate(blocks, axis=0)

    t2_re, t2_im = _pow_rows(hi_up_dn, _cpow(jnp.where(is_f_l, i_i, 8 - i_i).astype(F32), re, im))
    k_row = lax.broadcasted_iota(jnp.int32, (t_len, SQ), 0)
    lt = jnp.where(is_re_l, t2_re, t2_im)
    lt = jnp.where(jnp.logical_and(jnp.logical_not(is_f_l), k_row == 0), 0.0, lt)

    def taps_of(lanes):
        return lax.dot_general(dmat[:, lanes], lt[:, lanes], (((1,), (1,)), ((), ())),
                               preferred_element_type=F32, precision=lax.Precision.HIGHEST)

    k_f = taps_of(slice(0, half))
    k_b = taps_of(slice(half, SQ))
    in_b_re = jnp.logical_and(lane >= half, lane < half + SSM_STATE)
    kb0 = jnp.sum(jnp.where(in_b_re, dmat, 0.0), axis=1, keepdims=True)
    tap0 = lax.broadcasted_iota(jnp.int32, (SSM_GROUP * SSM_GROUP, t_len), 1) == 0
    k_f = k_f + jnp.where(tap0, kb0 + dcol_ref[0, 0], 0.0)
    taps = jnp.concatenate([k_f, k_b], axis=1).astype(BF16).astype(F32)
    bits = pltpu.bitcast(taps, jnp.int32)
    ktab_ref[0, 0] = pltpu.roll(bits, 1, axis=1) | lax.shift_right_logical(bits, 16)

    t3_re, t3_im = _pow_rows(hi_up_dn, _cpow(jnp.where(is_f_l, i_i + 1, 8 - i_i).astype(F32), re, im))
    for c in range(SSM_GROUP):
        cr = c_re[c:c + 1, :]
        ci = c_im[c:c + 1, :]
        g_re = cr * t3_re - ci * t3_im
        g_im = cr * t3_im + ci * t3_re
        wout_ref[0, 0, c * t_len:(c + 1) * t_len, :] = jnp.where(is_re_l, g_re, -g_im).astype(wout_ref.dtype)


def _ssm_prep(lam_re, lam_im, log_dt, b_re, b_im, c_re, c_im, d_skip):
    ns, _, n_g, n_p = lam_re.shape
    cg = SSM_GROUP

    def lanes(a):
        return jnp.concatenate([a[:, 0], a[:, 0], a[:, 1], a[:, 1]], axis=-1)

    ldt = jnp.broadcast_to(log_dt[..., None], lam_re.shape)
    rowp = jnp.stack([lanes(lam_re), lanes(lam_im), lanes(ldt)] + [jnp.zeros((ns, n_g, SQ), F32)] * 5, axis=2)
    bl_re = lanes(jnp.swapaxes(b_re, -1, -2))
    bl_im = lanes(jnp.swapaxes(b_im, -1, -2))
    cl_re = lanes(c_re)
    cl_im = lanes(c_im)
    dcol =(jnp.eye(cg, dtype=F32)[None, None] * d_skip.reshape(ns, n_g, 1, cg)).reshape(ns, n_g, cg * cg, 1)

    def spec(*shape):
        nd = len(shape)
        return pl.BlockSpec((1, 1) + shape, lambda j, g: (j, g) + (0,) * nd)

    return pl.pallas_call(
        _ssm_prep_kernel,
        grid=(ns, n_g),
        in_specs=[spec(8, SQ), spec(cg, SQ), spec(cg, SQ), spec(cg, SQ), spec(cg, SQ),
                  spec(cg * cg, 1)],
        out_specs=[spec(cg * cg, 2 * SSM_CHUNK), spec(GW, SQ), spec(GW, SQ), spec(8, SQ)],
        out_shape=[jax.ShapeDtypeStruct((ns, n_g, cg * cg, 2 * SSM_CHUNK), jnp.int32),
                   jax.ShapeDtypeStruct((ns, n_g, GW, SQ), BF16),
                   jax.ShapeDtypeStruct((ns, n_g, GW, SQ), BF16),
                   jax.ShapeDtypeStruct((ns, n_g, 8, SQ), F32)],
        compiler_params=_cparams(("parallel", "parallel")),
    )(rowp, bl_re, bl_im, cl_re, cl_im, dcol)


def _ssm_state_kernel(*refs, n_in):
    win_ref = refs[n_in]
    for x_ref, s_ref in zip(refs[:n_in], refs[n_in + 1:]):
        s_ref[0] = jnp.dot(x_ref[0].astype(BF16), win_ref[0], preferred_element_type=F32)


def _ssm_state(xgs, win, layer):
    n_g = win.shape[1]
    n_in = len(xgs)
    return pl.pallas_call(
        functools.partial(_ssm_state_kernel, n_in=n_in),
        grid=(n_g,),
        in_specs=[pl.BlockSpec((1, xg.shape[1], GW), lambda g: (g, 0, 0)) for xg in xgs]
        + [pl.BlockSpec((None, 1, GW, SQ), lambda g: (layer, g, 0, 0))],
        out_specs=[pl.BlockSpec((1, xg.shape[1], SQ), lambda g: (g, 0, 0)) for xg in xgs],
        out_shape=[jax.ShapeDtypeStruct((n_g, xg.shape[1], SQ), F32) for xg in xgs],
        compiler_params=_cparams(("parallel",)),
    )(*xgs, win)


SCAN_GROUPS = 8


def _ssm_scan_kernel(s_ref, a_ref, h_ref, *, n_chunks):
    half = SQ // 2
    nb = s_ref.shape[2]
    for k in range(s_ref.shape[0]):
        a1_f = a_ref[k, 0:1, 0:half]
        a2_f = a_ref[k, 1:2, 0:half]
        a1_b = a_ref[k, 0:1, half:]
        a2_b = a_ref[k, 1:2, half:]
        hf = jnp.zeros((nb, half), F32)
        hb = jnp.zeros((nb, half), F32)
        for i in range(n_chunks):
            j = n_chunks - 1 - i
            h_ref[k, i, :, 0:half] = hf.astype(h_ref.dtype)
            h_ref[k, j, :, half:] = hb.astype(h_ref.dtype)
            hf = a1_f * hf + a2_f * pltpu.roll(hf, SSM_STATE, axis=1) + s_ref[k, i, :, 0:half]
            hb = a1_b * hb + a2_b * pltpu.roll(hb, SSM_STATE, axis=1) + s_ref[k, j, :, half:]


def _ssm_scan(s, atab, layer):
    n_g, n_chunks, nb, _ = s.shape
    gb = SCAN_GROUPS if n_g % SCAN_GROUPS == 0 else 1
    return pl.pallas_call(
        functools.partial(_ssm_scan_kernel, n_chunks=n_chunks),
        grid=(n_g // gb,),
        in_specs=[pl.BlockSpec((gb, n_chunks, nb, SQ), lambda g: (g, 0, 0, 0)),
                  pl.BlockSpec((None, gb, 8, SQ), lambda g: (layer, g, 0, 0))],
        out_specs=pl.BlockSpec((gb, n_chunks, nb, SQ), lambda g: (g, 0, 0, 0)),
        out_shape=jax.ShapeDtypeStruct((n_g, n_chunks, nb, SQ), BF16),
        compiler_params=_cparams(("parallel",)),
    )(s, atab)


def _gelu_tanh(y):
    return 0.5 * y * (1.0 + jnp.tanh(math.sqrt(2.0 / math.pi) * (y + 0.044715 * (y * y * y))))


SSM_OUT_KSLICE = 256


def _ssm_out_kernel(*refs, n_in):
    x_refs = refs[:n_in]
    h_refs = refs[n_in:2 * n_in]
    ktab_ref, wout_ref = refs[2 * n_in:2 * n_in + 2]
    z_refs = refs[2 * n_in + 2:3 * n_in + 2]
    m_ref, xb_ref, y_ref = refs[3 * n_in + 2:]
    t_len = SSM_CHUNK
    offs = [0]
    for x_ref in x_refs:
        offs.append(offs[-1] + x_ref.shape[1])

    for x_ref, h_ref, lo, hi in zip(x_refs, h_refs, offs[:-1], offs[1:]):
        xb_ref[lo:hi, :] = x_ref[0].astype(BF16)
        y_ref[lo:hi, :] = lax.dot_general(h_ref[0], wout_ref[0], (((1,), (1,)), ((), ())),
                                          preferred_element_type=F32)

    for kk in range(GW // SSM_OUT_KSLICE):
        for ci in range(kk * SSM_OUT_KSLICE // t_len, (kk + 1) * SSM_OUT_KSLICE // t_len):
            for co in range(SSM_GROUP):
                r = ci * SSM_GROUP + co
                p8 = jnp.broadcast_to(ktab_ref[0, r:r + 1, :], (8, 2 * t_len))
                for a in range(t_len // 16):
                    w = pltpu.roll(p8, 16 * a, axis=1, stride=2, stride_axis=0)[:, :t_len]
                    m_ref[ci * t_len + 16 * a:ci * t_len + 16 * a + 16, co * t_len:(co + 1) * t_len] = (
                        pltpu.bitcast(w, m_ref.dtype))
        ks = slice(kk * SSM_OUT_KSLICE, (kk + 1) * SSM_OUT_KSLICE)
        y_ref[...] += jnp.dot(xb_ref[:, ks], m_ref[ks, :], preferred_element_type=F32)

    for z_ref, lo, hi in zip(z_refs, offs[:-1], offs[1:]):
        z_ref[0] = _gelu_tanh(y_ref[lo:hi, :]).astype(z_ref.dtype)


def _ssm_out(xgs, hprevs, ktab, wout, layer):
    n_g = ktab.shape[1]
    n_in = len(xgs)
    rows = sum(xg.shape[1] for xg in xgs)
    return pl.pallas_call(
        functools.partial(_ssm_out_kernel, n_in=n_in),
        grid=(n_g,),
        in_specs=[pl.BlockSpec((1, xg.shape[1], GW), lambda g: (g, 0, 0)) for xg in xgs]
        + [pl.BlockSpec((1, h.shape[1], SQ), lambda g: (g, 0, 0)) for h in hprevs]
        + [pl.BlockSpec((None, 1, SSM_GROUP * SSM_GROUP, 2 * SSM_CHUNK), lambda g: (layer, g, 0, 0)),
           pl.BlockSpec((None, 1, GW, SQ), lambda g: (layer, g, 0, 0))],
        out_specs=[pl.BlockSpec((1, xg.shape[1], GW), lambda g: (g, 0, 0)) for xg in xgs],
        out_shape=[jax.ShapeDtypeStruct((n_g, xg.shape[1], GW), F32) for xg in xgs],
        scratch_shapes=[pltpu.VMEM((GW, GW), BF16),
                        pltpu.VMEM((rows, GW), BF16),
                        pltpu.VMEM((rows, GW), F32)],
        compiler_params=_cparams(("parallel",)),
    )(*xgs, *hprevs, ktab, wout)


def _rope_tables(seq):
    inv_freq = ROPE_THETA ** (-jnp.arange(0, HEAD_DIM, 2, dtype=F32) / HEAD_DIM)
    ang = jnp.arange(seq, dtype=F32)[:, None] * inv_freq[None, :]
    cos, sin = jnp.cos(ang), jnp.sin(ang)
    return jnp.concatenate([cos, cos], axis=-1), jnp.concatenate([-sin, sin], axis=-1)


def kernel(x_prompt, x_sample, mem_prompt, mem_sample, ssm_w_in, ssm_lam_re, ssm_lam_im, ssm_log_dt, ssm_b_re,
           ssm_b_im, ssm_c_re, ssm_c_im, ssm_d, ssm_w_glu, attn_w_in, attn_sink, w_mem_kv, w_out, ln1_g, ln1_b,
           w_ff1, w_ff2, ln2_g, ln2_b):
    depth = w_out.shape[0]
    d_model = x_prompt.shape[-1]
    alpha = (2 * depth) ** 0.25
    mix_w = ssm_d.shape[-1]
    n_groups = mix_w // SSM_GROUP
    mem_w = N_MEM_HEADS * HEAD_DIM

    ssm_w_in_b = ssm_w_in.astype(BF16)
    ssm_w_glu_b = ssm_w_glu.astype(BF16)
    attn_w_in_b = attn_w_in.astype(BF16)
    w_mem_kv_b = w_mem_kv.astype(BF16)
    w_out_b = w_out.astype(BF16)
    w_ff1_b = w_ff1.astype(BF16)
    w_ff2_b = w_ff2.astype(BF16)

    ktab, win, wout, atab = _ssm_prep(ssm_lam_re, ssm_lam_im, ssm_log_dt, ssm_b_re, ssm_b_im,
                                      ssm_c_re, ssm_c_im, ssm_d.reshape(-1, n_groups, SSM_GROUP))
    shapes = [x_prompt.shape, x_sample.shape]
    ropes = [_rope_tables(s[1]) for s in shapes]
    xs = [x_prompt.reshape(-1, d_model), x_sample.reshape(-1, d_model)]
    mems = [mem_prompt, mem_sample]

    ln1_g3, ln1_b3, ln2_g3, ln2_b3 = [a.reshape(depth, 1, d_model) for a in (ln1_g, ln1_b, ln2_g, ln2_b)]

    for i in range(depth):
        j = i // 2
        kvs = [_rowmm(m.reshape(-1, d_model), w_mem_kv_b, i).reshape(m.shape[0], m.shape[1], -1) for m in mems]
        if i % 2 == 0:
            xgs, qms = zip(*[_ssm_inproj(x.reshape(s), ssm_w_in_b, j, mix_w) for x, s in zip(xs, shapes)])
            ss = _ssm_state(xgs, win, j)
            hprevs = [_ssm_scan(s.reshape(n_groups, -1, shp[0], SQ), atab, j).reshape(n_groups, -1, SQ)
                      for s, shp in zip(ss, shapes)]
            zs = _ssm_out(xgs, hprevs, ktab, wout, j)
            ymix = [_glu(z, ssm_w_glu_b, j, shp[0]) for z, shp in zip(zs, shapes)]
            ymem = [_memattn(q, 0, kv) for q, kv in zip(qms, kvs)]
        else:
            projs = [_attn_inproj(x, attn_w_in_b, j, cos2, sin2, s[1]).reshape(s[0], s[1], -1)
                     for x, s, (cos2, sin2) in zip(xs, shapes, ropes)]
            ymix = [_wattn(p, attn_sink[j]) for p in projs]
            qcol = (mix_w + 2 * N_KV_HEADS * HEAD_DIM) // mem_w
            ymem = [_memattn(p, qcol, kv) for p, kv in zip(projs, kvs)]
        xs = [_oproj(ym.reshape(-1, mix_w), ye.reshape(-1, mem_w), w_out_b, x, ln1_g3, ln1_b3, i, alpha)
              for ym, ye, x in zip(ymix, ymem, xs)]
        xs = [_ffn(x, w_ff1_b, w_ff2_b, ln2_g3, ln2_b3, i, alpha) for x in xs]

    return (xs[0].reshape(shapes[0]), xs[1].reshape(shapes[1]))
```

```python
import functools
import math

import jax
import jax.numpy as jnp
from jax import lax
from jax.experimental import pallas as pl
from jax.experimental.pallas import tpu as pltpu

F32 = jnp.float32
BF16 = jnp.bfloat16

HEAD_DIM = 128
N_KV_HEADS = 4
GQA_GROUP = 3
WINDOW = 128
N_MEM_HEADS = 4
SSM_GROUP = 16
SSM_STATE = 64
SSM_CHUNK = 128
ROPE_THETA = 10000.0
LN_EPS = 1e-5
NEG_INF = -1e30
SQ = 2 * SSM_STATE * 2
GW = SSM_GROUP * SSM_CHUNK
V7X_VMEM_LIMIT_MB = 56


def _cparams(sem, vmem_mb=V7X_VMEM_LIMIT_MB):
    return pltpu.CompilerParams(dimension_semantics=sem, vmem_limit_bytes=vmem_mb * 1024 * 1024)


def _tile(n, target):
    if n <= target:
        return n
    for t in range(target, 7, -1):
        if n % t == 0 and t % 8 == 0:
            return t
    return n


def _layer_norm(y, g, b):
    mu = jnp.mean(y, axis=-1, keepdims=True)
    d = y - mu
    var = jnp.mean(d * d, axis=-1, keepdims=True)
    return d * lax.rsqrt(var + LN_EPS) * g + b


def _rowmm_kernel(x_ref, w_ref, o_ref, *, n_chunk):
    xb = x_ref[...].astype(BF16)
    n = w_ref.shape[1]
    for n0 in range(0, n, n_chunk):
        o_ref[:, n0:n0 + n_chunk] = jnp.dot(
            xb, w_ref[:, n0:n0 + n_chunk], preferred_element_type=F32).astype(o_ref.dtype)


def _rowmm(x, w, layer, tm_target=512):
    m, k = x.shape
    n = w.shape[2]
    tm = _tile(m, tm_target)
    n_chunk = 512 if n % 512 == 0 else n
    return pl.pallas_call(
        functools.partial(_rowmm_kernel, n_chunk=n_chunk),
        grid=(m // tm,),
        in_specs=[pl.BlockSpec((tm, k), lambda i: (i, 0)),
                  pl.BlockSpec((None, k, n), lambda i: (layer, 0, 0), pipeline_mode=pl.Buffered(1))],
        out_specs=pl.BlockSpec((tm, n), lambda i: (i, 0)),
        out_shape=jax.ShapeDtypeStruct((m, n), BF16),
        compiler_params=_cparams(("parallel",)),
    )(x, w)


RELAYOUT_ROWS = 8
CH_PASS = 512
SLAB_PITCH = CH_PASS + 8


def _chunks_per_step(bsz):
    assert RELAYOUT_ROWS % bsz == 0, "batch must divide the 8-row relayout tile"
    return RELAYOUT_ROWS // bsz


def _ssm_inproj_kernel(x_ref, wu_ref, wq_ref, xg_ref, q_ref, tt_ref, *, bsz, cpb):
    d = x_ref.shape[-1]
    xb = x_ref[...].reshape(RELAYOUT_ROWS * SSM_CHUNK, d).astype(BF16)
    q = jnp.dot(xb, wq_ref[...], preferred_element_type=F32)
    q_ref[...] = q.reshape(q_ref.shape).astype(q_ref.dtype)
    for n in range(wu_ref.shape[1] // CH_PASS):
        u = jnp.dot(xb, wu_ref[:, n * CH_PASS:(n + 1) * CH_PASS], preferred_element_type=F32)
        for b in range(bsz):
            for cc in range(cpb):
                src = (b * cpb + cc) * SSM_CHUNK
                slot = cc * bsz + b
                tt_ref[slot * SLAB_PITCH:slot * SLAB_PITCH + CH_PASS, :] = u[src:src + SSM_CHUNK, :].T
        for r in range(CH_PASS):
            g = (n * CH_PASS + r) // SSM_GROUP
            c = r % SSM_GROUP
            xg_ref[g, :, c * SSM_CHUNK:(c + 1) * SSM_CHUNK] = tt_ref[pl.ds(r, RELAYOUT_ROWS, stride=SLAB_PITCH), :]


def _ssm_inproj(x, w_in, layer, mix_w):
    bsz, seq, d = x.shape
    nc = seq // SSM_CHUNK
    cpb = _chunks_per_step(bsz)
    assert nc % cpb == 0 and mix_w % CH_PASS == 0
    n_g = mix_w // SSM_GROUP
    mem_w = w_in.shape[2] - mix_w
    x4 = x.reshape(bsz, nc, SSM_CHUNK, d)
    return pl.pallas_call(
        functools.partial(_ssm_inproj_kernel, bsz=bsz, cpb=cpb),
        grid=(nc // cpb,),
        in_specs=[pl.BlockSpec((bsz, cpb, SSM_CHUNK, d), lambda i: (0, i, 0, 0)),
                  pl.BlockSpec((None, d, mix_w), lambda i: (layer, 0, 0), pipeline_mode=pl.Buffered(1)),
                  pl.BlockSpec((None, d, mem_w), lambda i: (layer, 0, mix_w // mem_w),
                               pipeline_mode=pl.Buffered(1))],
        out_specs=[pl.BlockSpec((n_g, RELAYOUT_ROWS, GW), lambda i: (0, i, 0)),
                   pl.BlockSpec((bsz, cpb * SSM_CHUNK, mem_w), lambda i: (0, i, 0))],
        out_shape=[jax.ShapeDtypeStruct((n_g, nc * bsz, GW), F32),
                   jax.ShapeDtypeStruct((bsz, seq, mem_w), BF16)],
        scratch_shapes=[pltpu.VMEM((RELAYOUT_ROWS * SLAB_PITCH, SSM_CHUNK), F32)],
        compiler_params=_cparams(("parallel",)),
    )(x4, w_in, w_in)


def _glu_kernel(z_ref, wa_ref, wg_ref, o_ref, tt_ref, zt_ref, *, bsz, cpb):
    n_out = wa_ref.shape[1]
    for n in range(zt_ref.shape[1] // CH_PASS):
        for r in range(CH_PASS):
            g = (n * CH_PASS + r) // SSM_GROUP
            c = r % SSM_GROUP
            tt_ref[pl.ds(r, RELAYOUT_ROWS, stride=SLAB_PITCH), :] = z_ref[g, :, c * SSM_CHUNK:(c + 1) * SSM_CHUNK]
        for b in range(bsz):
            for cc in range(cpb):
                dst = (b * cpb + cc) * SSM_CHUNK
                slot = cc * bsz + b
                zt_ref[dst:dst + SSM_CHUNK, n * CH_PASS:(n + 1) * CH_PASS] = (
                    tt_ref[slot * SLAB_PITCH:slot * SLAB_PITCH + CH_PASS, :].T.astype(zt_ref.dtype))
    z = zt_ref[...]
    for n0 in range(0, n_out, CH_PASS):
        a = jnp.dot(z, wa_ref[:, n0:n0 + CH_PASS], preferred_element_type=F32)
        g = jnp.dot(z, wg_ref[:, n0:n0 + CH_PASS], preferred_element_type=F32)
        y = (a * jax.nn.sigmoid(g)).astype(o_ref.dtype)
        o_ref[:, :, n0:n0 + CH_PASS] = y.reshape(bsz, cpb * SSM_CHUNK, CH_PASS)


def _glu(z, w_glu, layer, bsz):
    n_g, rows, _ = z.shape
    mix_w = n_g * SSM_GROUP
    cpb = _chunks_per_step(bsz)
    nc = rows // bsz
    return pl.pallas_call(
        functools.partial(_glu_kernel, bsz=bsz, cpb=cpb),
        grid=(nc // cpb,),
        in_specs=[pl.BlockSpec((n_g, RELAYOUT_ROWS, GW), lambda i: (0, i, 0)),
                  pl.BlockSpec((None, mix_w, mix_w), lambda i: (layer, 0, 0), pipeline_mode=pl.Buffered(1)),
                  pl.BlockSpec((None, mix_w, mix_w), lambda i: (layer, 0, 1), pipeline_mode=pl.Buffered(1))],
        out_specs=pl.BlockSpec((bsz, cpb * SSM_CHUNK, mix_w), lambda i: (0, i, 0)),
        out_shape=jax.ShapeDtypeStruct((bsz, nc * SSM_CHUNK, mix_w), BF16),
        scratch_shapes=[pltpu.VMEM((RELAYOUT_ROWS * SLAB_PITCH, SSM_CHUNK), F32),
                        pltpu.VMEM((RELAYOUT_ROWS * SSM_CHUNK, mix_w), BF16)],
        compiler_params=_cparams(("parallel",)),
    )(z, w_glu, w_glu)


def _oproj_kernel(ymix_ref, ymem_ref, wa_ref, wb_ref, x_ref, g_ref, b_ref, o_ref, *, alpha, n_sub):
    sub = x_ref.shape[0] // n_sub
    for r in range(n_sub):
        rs = slice(r * sub, (r + 1) * sub)
        acc = jnp.dot(ymix_ref[rs, :], wa_ref[...], preferred_element_type=F32)
        acc = acc + jnp.dot(ymem_ref[rs, :], wb_ref[...], preferred_element_type=F32)
        y = alpha * x_ref[rs, :] + acc
        o_ref[rs, :] = _layer_norm(y, g_ref[...], b_ref[...])


def _oproj(ymix, ymem, w_out, x, g, b, layer, alpha, tm_target=1024, n_sub=4):
    m, d = x.shape
    kmix = ymix.shape[1]
    kmem = ymem.shape[1]
    tm = _tile(m, tm_target)
    if tm % (8 * n_sub):
        n_sub = 1
    return pl.pallas_call(
        functools.partial(_oproj_kernel, alpha=alpha, n_sub=n_sub),
        grid=(m // tm,),
        in_specs=[pl.BlockSpec((tm, kmix), lambda i: (i, 0)),
                  pl.BlockSpec((tm, kmem), lambda i: (i, 0)),
                  pl.BlockSpec((None, kmix, d), lambda i: (layer, 0, 0), pipeline_mode=pl.Buffered(1)),
                  pl.BlockSpec((None, kmem, d), lambda i: (layer, kmix // kmem, 0), pipeline_mode=pl.Buffered(1)),
                  pl.BlockSpec((tm, d), lambda i: (i, 0)),
                  pl.BlockSpec((None, 1, d), lambda i: (layer, 0, 0)),
                  pl.BlockSpec((None, 1, d), lambda i: (layer, 0, 0))],
        out_specs=pl.BlockSpec((tm, d), lambda i: (i, 0)),
        out_shape=jax.ShapeDtypeStruct((m, d), F32),
        compiler_params=_cparams(("parallel",)),
    )(ymix, ymem, w_out, w_out, x, g, b)


def _ffn_kernel(x_ref, w1_ref, w2_ref, g_ref, b_ref, o_ref, xb_ref, acc_ref, *, alpha, n_sub):
    f = pl.program_id(1)
    last = pl.num_programs(1) - 1

    def ffn_part(xb):
        h = jnp.dot(xb, w1_ref[...], preferred_element_type=F32)
        h = jnp.square(jnp.maximum(h, 0.0)).astype(BF16)
        return jnp.dot(h, w2_ref[...], preferred_element_type=F32)

    @pl.when(f == 0)
    def _():
        xb = x_ref[...].astype(BF16)
        xb_ref[...] = xb
        acc_ref[...] = alpha * x_ref[...] + ffn_part(xb)

    @pl.when(jnp.logical_and(f > 0, f < last))
    def _():
        acc_ref[...] += ffn_part(xb_ref[...])

    @pl.when(f == last)
    def _():
        sub = acc_ref.shape[0] // n_sub
        for r in range(n_sub):
            rows = slice(r * sub, (r + 1) * sub)
            o_ref[rows, :] = _layer_norm(acc_ref[rows, :] + ffn_part(xb_ref[rows, :]), g_ref[...], b_ref[...])


def _ffn(x, w1, w2, g, b, layer, alpha, tm_target=512, tf_target=1024, n_sub=2):
    m, d = x.shape
    dff = w1.shape[2]
    tm = _tile(m, tm_target)
    tf = _tile(dff, tf_target)
    assert dff // tf >= 2, "the kernel has distinct first and last d_ff slices"
    return pl.pallas_call(
        functools.partial(_ffn_kernel, alpha=alpha, n_sub=n_sub if tm % (16 * n_sub) == 0 else 1),
        grid=(m // tm, dff // tf),
        in_specs=[pl.BlockSpec((tm, d), lambda i, f: (i, 0)),
                  pl.BlockSpec((None, d, tf), lambda i, f: (layer, 0, f)),
                  pl.BlockSpec((None, tf, d), lambda i, f: (layer, f, 0)),
                  pl.BlockSpec((None, 1, d), lambda i, f: (layer, 0, 0)),
                  pl.BlockSpec((None, 1, d), lambda i, f: (layer, 0, 0))],
        out_specs=pl.BlockSpec((tm, d), lambda i, f: (i, 0)),
        out_shape=jax.ShapeDtypeStruct((m, d), F32),
        scratch_shapes=[pltpu.VMEM((tm, d), BF16), pltpu.VMEM((tm, d), F32)],
        compiler_params=_cparams(("parallel", "arbitrary")),
    )(x, w1, w2, g, b)


def _memattn_kernel(q_ref, k_ref, v_ref, o_ref):
    scale = HEAD_DIM ** -0.5
    for h in range(N_MEM_HEADS):
        sl = slice(h * HEAD_DIM, (h + 1) * HEAD_DIM)
        s = lax.dot_general(q_ref[0, :, sl], k_ref[0, :, sl], (((1,), (1,)), ((), ())),
                            preferred_element_type=F32) * scale
        m = jnp.max(s, axis=-1, keepdims=True)
        p = jnp.exp(s - m)
        l = jnp.sum(p, axis=-1, keepdims=True)
        o = jnp.dot(p.astype(BF16), v_ref[0, :, sl], preferred_element_type=F32)
        o_ref[0, :, sl] = (o / l).astype(o_ref.dtype)


def _memattn(proj, q_col_block, kv, tq_target=1024):
    bsz, seq, _ = proj.shape
    n_mem = kv.shape[1]
    w = N_MEM_HEADS * HEAD_DIM
    tq = _tile(seq, tq_target)
    return pl.pallas_call(
        _memattn_kernel,
        grid=(bsz, seq // tq),
        in_specs=[pl.BlockSpec((1, tq, w), lambda b, i: (b, i, q_col_block)),
                  pl.BlockSpec((1, n_mem, w), lambda b, i: (b, 0, 0)),
                  pl.BlockSpec((1, n_mem, w), lambda b, i: (b, 0, 1))],
        out_specs=pl.BlockSpec((1, tq, w), lambda b, i: (b, i, 0)),
        out_shape=jax.ShapeDtypeStruct((bsz, seq, w), BF16),
        compiler_params=_cparams(("parallel", "parallel")),
    )(proj, kv, kv)


def _rope(x, cos2, sin2):
    return x * cos2 + pltpu.roll(x, HEAD_DIM // 2, axis=1) * sin2


def _wattn_kernel(sink_ref, q_ref, kp_ref, kc_ref, kn_ref, vp_ref, vc_ref, vn_ref,
                  o_ref, kr_ref, vr_ref, *, tq, seq):
    i = pl.program_id(1)
    base = i * tq
    blk = WINDOW
    nqb = tq // blk

    kr_ref[0:blk, :] = kp_ref[0]
    kr_ref[blk:blk + tq, :] = kc_ref[0]
    kr_ref[blk + tq:, :] = kn_ref[0]
    ones = jnp.ones((tq + 2 * blk, HEAD_DIM), BF16)
    for h in range(N_KV_HEADS):
        src = slice(h * HEAD_DIM, (h + 1) * HEAD_DIM)
        dst = slice(2 * h * HEAD_DIM, (2 * h + 1) * HEAD_DIM)
        vr_ref[0:blk, dst] = vp_ref[0, :, src]
        vr_ref[blk:blk + tq, dst] = vc_ref[0, :, src]
        vr_ref[blk + tq:, dst] = vn_ref[0, :, src]
        vr_ref[:, (2 * h + 1) * HEAD_DIM:(2 * h + 2) * HEAD_DIM] = ones

    r = lax.broadcasted_iota(jnp.int32, (blk, 3 * blk), 0)
    c = lax.broadcasted_iota(jnp.int32, (blk, 3 * blk), 1)
    rel = c - blk - r
    band = jnp.abs(rel) <= WINDOW

    for jb in range(nqb):
        kpos = base + (jb - 1) * blk + c
        valid = band & (kpos >= 0) & (kpos < seq)
        bias = jnp.where(valid, 0.0, NEG_INF).astype(F32)
        for hk in range(N_KV_HEADS):
            ksl = slice(hk * HEAD_DIM, (hk + 1) * HEAD_DIM)
            kw = kr_ref[jb * blk:(jb + 3) * blk, ksl]
            vw = vr_ref[jb * blk:(jb + 3) * blk, 2 * hk * HEAD_DIM:(2 * hk + 2) * HEAD_DIM]
            qs = jnp.concatenate(
                [q_ref[0, jb * blk:(jb + 1) * blk, (hk * GQA_GROUP + g) * HEAD_DIM:(hk * GQA_GROUP + g + 1) * HEAD_DIM]
                 for g in range(GQA_GROUP)], axis=0)
            s = lax.dot_general(qs, kw, (((1,), (1,)), ((), ())), preferred_element_type=F32)
            ps = []
            sink_terms = []
            for g in range(GQA_GROUP):
                sk = sink_ref[hk * GQA_GROUP + g]
                sg = s[g * blk:(g + 1) * blk, :] + bias
                m = jnp.maximum(jnp.max(sg, axis=-1, keepdims=True), sk)
                ps.append(jnp.exp(sg - m).astype(BF16))
                sink_terms.append(jnp.exp(sk - m))
            pcat = jnp.concatenate(ps, axis=0)
            o = jnp.dot(pcat, vw, preferred_element_type=F32)
            for g in range(GQA_GROUP):
                hq = hk * GQA_GROUP + g
                den = o[g * blk:(g + 1) * blk, HEAD_DIM:] + sink_terms[g]
                o_ref[0, jb * blk:(jb + 1) * blk, hq * HEAD_DIM:(hq + 1) * HEAD_DIM] = (
                    o[g * blk:(g + 1) * blk, :HEAD_DIM] / den).astype(o_ref.dtype)


def _attn_inproj_kernel(x_ref, w_ref, cos_ref, sin_ref, o_ref, *, n_q_heads, n_rope_heads, n_chunk):
    xb = x_ref[...].astype(BF16)
    cos2 = cos_ref[...]
    sin2 = sin_ref[...]
    scale = HEAD_DIM ** -0.5
    for n0 in range(0, w_ref.shape[1], n_chunk):
        y = jnp.dot(xb, w_ref[:, n0:n0 + n_chunk], preferred_element_type=F32)
        for c0 in range(0, n_chunk, HEAD_DIM):
            head = (n0 + c0) // HEAD_DIM
            blk = y[:, c0:c0 + HEAD_DIM]
            if head < n_rope_heads:
                blk = _rope(blk, cos2, sin2)
            if head < n_q_heads:
                blk = blk * scale
            o_ref[:, n0 + c0:n0 + c0 + HEAD_DIM] = blk.astype(o_ref.dtype)


def _attn_inproj(x, w, layer, cos2, sin2, seq, tm_target=512):
    m, k = x.shape
    n = w.shape[2]
    tm = _tile(seq, tm_target)
    n_q_heads = N_KV_HEADS * GQA_GROUP
    return pl.pallas_call(
        functools.partial(_attn_inproj_kernel, n_q_heads=n_q_heads, n_rope_heads=n_q_heads + N_KV_HEADS,
                          n_chunk=4 * HEAD_DIM),
        grid=(m // tm,),
        in_specs=[pl.BlockSpec((tm, k), lambda i: (i, 0)),
                  pl.BlockSpec((None, k, n), lambda i: (layer, 0, 0), pipeline_mode=pl.Buffered(1)),
                  pl.BlockSpec((tm, HEAD_DIM), lambda i: (i % (seq // tm), 0)),
                  pl.BlockSpec((tm, HEAD_DIM), lambda i: (i % (seq // tm), 0))],
        out_specs=pl.BlockSpec((tm, n), lambda i: (i, 0)),
        out_shape=jax.ShapeDtypeStruct((m, n), BF16),
        compiler_params=_cparams(("parallel",)),
    )(x, w, cos2, sin2)


def _wattn(proj, sink, tq_target=512):
    bsz, seq, _ = proj.shape
    blk = WINDOW
    qw = N_KV_HEADS * GQA_GROUP * HEAD_DIM
    kvw = N_KV_HEADS * HEAD_DIM
    tq = _tile(seq, tq_target)
    r = tq // blk
    nb = seq // blk
    kcol = qw // kvw
    vcol = kcol + 1
    in_specs = [
        pl.BlockSpec(memory_space=pltpu.SMEM),
        pl.BlockSpec((1, tq, qw), lambda b, i: (b, i, 0)),
        pl.BlockSpec((1, blk, kvw), lambda b, i: (b, jnp.maximum(i * r - 1, 0), kcol)),
        pl.BlockSpec((1, tq, kvw), lambda b, i: (b, i, kcol)),
        pl.BlockSpec((1, blk, kvw), lambda b, i: (b, jnp.minimum(i * r + r, nb - 1), kcol)),
        pl.BlockSpec((1, blk, kvw), lambda b, i: (b, jnp.maximum(i * r - 1, 0), vcol)),
        pl.BlockSpec((1, tq, kvw), lambda b, i: (b, i, vcol)),
        pl.BlockSpec((1, blk, kvw), lambda b, i: (b, jnp.minimum(i * r + r, nb - 1), vcol)),
    ]
    return pl.pallas_call(
        functools.partial(_wattn_kernel, tq=tq, seq=seq),
        grid=(bsz, seq // tq),
        in_specs=in_specs,
        out_specs=pl.BlockSpec((1, tq, qw), lambda b, i: (b, i, 0)),
        out_shape=jax.ShapeDtypeStruct((bsz, seq, qw), BF16),
        scratch_shapes=[pltpu.VMEM((tq + 2 * blk, kvw), BF16),
                        pltpu.VMEM((tq + 2 * blk, 2 * kvw), BF16)],
        compiler_params=_cparams(("parallel", "parallel")),
    )(sink, proj, proj, proj, proj, proj, proj, proj)


def _cpow(e, re, im):
    mg = jnp.exp(e * re)
    return mg * jnp.cos(e * im), mg * jnp.sin(e * im)


def _pow_rows(hi, lo):
    (hi_re, hi_im), (lo_re, lo_im) = hi, lo
    rows_re, rows_im = [], []
    for a in range(hi_re.shape[0]):
        ar = hi_re[a:a + 1, :]
        ai = hi_im[a:a + 1, :]
        rows_re.append(ar * lo_re - ai * lo_im)
        rows_im.append(ar * lo_im + ai * lo_re)
    return jnp.concatenate(rows_re, axis=0), jnp.concatenate(rows_im, axis=0)


def _ssm_prep_kernel(rowp_ref, blre_ref, blim_ref, clre_ref, clim_ref, dcol_ref,
                     ktab_ref, win_ref, wout_ref, atab_ref):
    t_len = SSM_CHUNK
    half = SQ // 2
    lane = lax.broadcasted_iota(jnp.int32, (1, SQ), 1)
    is_re_l = (lane % half) < SSM_STATE
    is_f_l = lane < half

    lam_re = rowp_ref[0, 0, 0:1, :]
    lam_im = rowp_ref[0, 0, 1:2, :]
    dt = jnp.exp(rowp_ref[0, 0, 2:3, :])
    re = lam_re * dt
    im = lam_im * dt
    mag = jnp.exp(re)
    n_re = mag * jnp.cos(im) - 1.0
    n_im = mag * jnp.sin(im)
    den = lam_re * lam_re + lam_im * lam_im
    co_re = (n_re * lam_re + n_im * lam_im) / den
    co_im = (n_im * lam_re - n_re * lam_im) / den
    b_re = blre_ref[0, 0]
    b_im = blim_ref[0, 0]
    bb_re = co_re * b_re - co_im * b_im
    bb_im = co_re * b_im + co_im * b_re

    a_i = lax.broadcasted_iota(jnp.int32, (t_len // 8, SQ), 0)
    i_i = lax.broadcasted_iota(jnp.int32, (8, SQ), 0)
    up, dn = 8 * a_i, t_len - 8 - 8 * a_i
    hi_dn_up = _cpow(jnp.where(is_f_l, dn, up).astype(F32), re, im)
    hi_up_dn = _cpow(jnp.where(is_f_l, up, dn).astype(F32), re, im)
    l_re, l_im = _pow_rows(hi_dn_up, _cpow(jnp.where(is_f_l, 7 - i_i, i_i).astype(F32), re, im))
    p1 = jnp.where(is_re_l, bb_re, bb_im)
    p2 = jnp.where(is_re_l, -bb_im, bb_re)
    for c in range(SSM_GROUP):
        win_ref[0, 0, c * t_len:(c + 1) * t_len, :] = (
            l_re * p1[c:c + 1, :] + l_im * p2[c:c + 1, :]).astype(win_ref.dtype)

    mg_t = jnp.exp(t_len * re)
    a_re = mg_t * jnp.cos(t_len * im)
    a_im = mg_t * jnp.sin(t_len * im)
    atab_ref[0, 0, 0:1, :] = a_re
    atab_ref[0, 0, 1:2, :] = jnp.where(is_re_l, -a_im, a_im)
    atab_ref[0, 0, 2:8, :] = jnp.zeros((6, SQ), F32)

    c_re = clre_ref[0, 0]
    c_im = clim_ref[0, 0]
    blocks = []
    for ci in range(SSM_GROUP):
        br = bb_re[ci:ci + 1, :]
        bi = bb_im[ci:ci + 1, :]
        d_re = c_re * br - c_im * bi
        d_im = c_re * bi + c_im * br
        blocks.append(jnp.where(is_re_l, d_re, -d_im))
    dmat = jnp.concatenate(blocks, axis=0)

    t2_re, t2_im = _pow_rows(hi_up_dn, _cpow(jnp.where(is_f_l, i_i, 8 - i_i).astype(F32), re, im))
    k_row = lax.broadcasted_iota(jnp.int32, (t_len, SQ), 0)
    lt = jnp.where(is_re_l, t2_re, t2_im)
    lt = jnp.where(jnp.logical_and(jnp.logical_not(is_f_l), k_row == 0), 0.0, lt)

    def taps_of(lanes):
        return lax.dot_general(dmat[:, lanes], lt[:, lanes], (((1,), (1,)), ((), ())),
                               preferred_element_type=F32, precision=lax.Precision.HIGHEST)

    k_f = taps_of(slice(0, half))
    k_b = taps_of(slice(half, SQ))
    in_b_re = jnp.logical_and(lane >= half, lane < half + SSM_STATE)
    kb0 = jnp.sum(jnp.where(in_b_re, dmat, 0.0), axis=1, keepdims=True)
    tap0 = lax.broadcasted_iota(jnp.int32, (SSM_GROUP * SSM_GROUP, t_len), 1) == 0
    k_f = k_f + jnp.where(tap0, kb0 + dcol_ref[0, 0], 0.0)
    taps = jnp.concatenate([k_f, k_b], axis=1).astype(BF16).astype(F32)
    bits = pltpu.bitcast(taps, jnp.int32)
    ktab_ref[0, 0] = pltpu.roll(bits, 1, axis=1) | lax.shift_right_logical(bits, 16)

    t3_re, t3_im = _pow_rows(hi_up_dn, _cpow(jnp.where(is_f_l, i_i + 1, 8 - i_i).astype(F32), re, im))
    for c in range(SSM_GROUP):
        cr = c_re[c:c + 1, :]
        ci = c_im[c:c + 1, :]
        g_re = cr * t3_re - ci * t3_im
        g_im = cr * t3_im + ci * t3_re
        wout_ref[0, 0, c * t_len:(c + 1) * t_len, :] = jnp.where(is_re_l, g_re, -g_im).astype(wout_ref.dtype)


def _ssm_prep(lam_re, lam_im, log_dt, b_re, b_im, c_re, c_im, d_skip):
    ns, _, n_g, n_p = lam_re.shape
    cg = SSM_GROUP

    def lanes(a):
        return jnp.concatenate([a[:, 0], a[:, 0], a[:, 1], a[:, 1]], axis=-1)

    ldt = jnp.broadcast_to(log_dt[..., None], lam_re.shape)
    rowp = jnp.stack([lanes(lam_re), lanes(lam_im), lanes(ldt)] + [jnp.zeros((ns, n_g, SQ), F32)] * 5, axis=2)
    bl_re = lanes(jnp.swapaxes(b_re, -1, -2))
    bl_im = lanes(jnp.swapaxes(b_im, -1, -2))
    cl_re = lanes(c_re)
    cl_im = lanes(c_im)
    dcol =(jnp.eye(cg, dtype=F32)[None, None] * d_skip.reshape(ns, n_g, 1, cg)).reshape(ns, n_g, cg * cg, 1)

    def spec(*shape):
        nd = len(shape)
        return pl.BlockSpec((1, 1) + shape, lambda j, g: (j, g) + (0,) * nd)

    return pl.pallas_call(
        _ssm_prep_kernel,
        grid=(ns, n_g),
        in_specs=[spec(8, SQ), spec(cg, SQ), spec(cg, SQ), spec(cg, SQ), spec(cg, SQ),
                  spec(cg * cg, 1)],
        out_specs=[spec(cg * cg, 2 * SSM_CHUNK), spec(GW, SQ), spec(GW, SQ), spec(8, SQ)],
        out_shape=[jax.ShapeDtypeStruct((ns, n_g, cg * cg, 2 * SSM_CHUNK), jnp.int32),
                   jax.ShapeDtypeStruct((ns, n_g, GW, SQ), BF16),
                   jax.ShapeDtypeStruct((ns, n_g, GW, SQ), BF16),
                   jax.ShapeDtypeStruct((ns, n_g, 8, SQ), F32)],
        compiler_params=_cparams(("parallel", "parallel")),
    )(rowp, bl_re, bl_im, cl_re, cl_im, dcol)


def _ssm_state_kernel(*refs, n_in):
    win_ref = refs[n_in]
    for x_ref, s_ref in zip(refs[:n_in], refs[n_in + 1:]):
        s_ref[0] = jnp.dot(x_ref[0].astype(BF16), win_ref[0], preferred_element_type=F32)


def _ssm_state(xgs, win, layer):
    n_g = win.shape[1]
    n_in = len(xgs)
    return pl.pallas_call(
        functools.partial(_ssm_state_kernel, n_in=n_in),
        grid=(n_g,),
        in_specs=[pl.BlockSpec((1, xg.shape[1], GW), lambda g: (g, 0, 0)) for xg in xgs]
        + [pl.BlockSpec((None, 1, GW, SQ), lambda g: (layer, g, 0, 0))],
        out_specs=[pl.BlockSpec((1, xg.shape[1], SQ), lambda g: (g, 0, 0)) for xg in xgs],
        out_shape=[jax.ShapeDtypeStruct((n_g, xg.shape[1], SQ), F32) for xg in xgs],
        compiler_params=_cparams(("parallel",)),
    )(*xgs, win)


SCAN_GROUPS = 8


def _ssm_scan_kernel(s_ref, a_ref, h_ref, *, n_chunks):
    half = SQ // 2
    nb = s_ref.shape[2]
    for k in range(s_ref.shape[0]):
        a1_f = a_ref[k, 0:1, 0:half]
        a2_f = a_ref[k, 1:2, 0:half]
        a1_b = a_ref[k, 0:1, half:]
        a2_b = a_ref[k, 1:2, half:]
        hf = jnp.zeros((nb, half), F32)
        hb = jnp.zeros((nb, half), F32)
        for i in range(n_chunks):
            j = n_chunks - 1 - i
            h_ref[k, i, :, 0:half] = hf.astype(h_ref.dtype)
            h_ref[k, j, :, half:] = hb.astype(h_ref.dtype)
            hf = a1_f * hf + a2_f * pltpu.roll(hf, SSM_STATE, axis=1) + s_ref[k, i, :, 0:half]
            hb = a1_b * hb + a2_b * pltpu.roll(hb, SSM_STATE, axis=1) + s_ref[k, j, :, half:]


def _ssm_scan(s, atab, layer):
    n_g, n_chunks, nb, _ = s.shape
    gb = SCAN_GROUPS if n_g % SCAN_GROUPS == 0 else 1
    return pl.pallas_call(
        functools.partial(_ssm_scan_kernel, n_chunks=n_chunks),
        grid=(n_g // gb,),
        in_specs=[pl.BlockSpec((gb, n_chunks, nb, SQ), lambda g: (g, 0, 0, 0)),
                  pl.BlockSpec((None, gb, 8, SQ), lambda g: (layer, g, 0, 0))],
        out_specs=pl.BlockSpec((gb, n_chunks, nb, SQ), lambda g: (g, 0, 0, 0)),
        out_shape=jax.ShapeDtypeStruct((n_g, n_chunks, nb, SQ), BF16),
        compiler_params=_cparams(("parallel",)),
    )(s, atab)


def _gelu_tanh(y):
    return 0.5 * y * (1.0 + jnp.tanh(math.sqrt(2.0 / math.pi) * (y + 0.044715 * (y * y * y))))


SSM_OUT_KSLICE = 256


def _ssm_out_kernel(*refs, n_in):
    x_refs = refs[:n_in]
    h_refs = refs[n_in:2 * n_in]
    ktab_ref, wout_ref = refs[2 * n_in:2 * n_in + 2]
    z_refs = refs[2 * n_in + 2:3 * n_in + 2]
    m_ref, xb_ref, y_ref = refs[3 * n_in + 2:]
    t_len = SSM_CHUNK
    offs = [0]
    for x_ref in x_refs:
        offs.append(offs[-1] + x_ref.shape[1])

    for x_ref, h_ref, lo, hi in zip(x_refs, h_refs, offs[:-1], offs[1:]):
        xb_ref[lo:hi, :] = x_ref[0].astype(BF16)
        y_ref[lo:hi, :] = lax.dot_general(h_ref[0], wout_ref[0], (((1,), (1,)), ((), ())),
                                          preferred_element_type=F32)

    for kk in range(GW // SSM_OUT_KSLICE):
        for ci in range(kk * SSM_OUT_KSLICE // t_len, (kk + 1) * SSM_OUT_KSLICE // t_len):
            for co in range(SSM_GROUP):
                r = ci * SSM_GROUP + co
                p8 = jnp.broadcast_to(ktab_ref[0, r:r + 1, :], (8, 2 * t_len))
                for a in range(t_len // 16):
                    w = pltpu.roll(p8, 16 * a, axis=1, stride=2, stride_axis=0)[:, :t_len]
                    m_ref[ci * t_len + 16 * a:ci * t_len + 16 * a + 16, co * t_len:(co + 1) * t_len] = (
                        pltpu.bitcast(w, m_ref.dtype))
        ks = slice(kk * SSM_OUT_KSLICE, (kk + 1) * SSM_OUT_KSLICE)
        y_ref[...] += jnp.dot(xb_ref[:, ks], m_ref[ks, :], preferred_element_type=F32)

    for z_ref, lo, hi in zip(z_refs, offs[:-1], offs[1:]):
        z_ref[0] = _gelu_tanh(y_ref[lo:hi, :]).astype(z_ref.dtype)


def _ssm_out(xgs, hprevs, ktab, wout, layer):
    n_g = ktab.shape[1]
    n_in = len(xgs)
    rows = sum(xg.shape[1] for xg in xgs)
    return pl.pallas_call(
        functools.partial(_ssm_out_kernel, n_in=n_in),
        grid=(n_g,),
        in_specs=[pl.BlockSpec((1, xg.shape[1], GW), lambda g: (g, 0, 0)) for xg in xgs]
        + [pl.BlockSpec((1, h.shape[1], SQ), lambda g: (g, 0, 0)) for h in hprevs]
        + [pl.BlockSpec((None, 1, SSM_GROUP * SSM_GROUP, 2 * SSM_CHUNK), lambda g: (layer, g, 0, 0)),
           pl.BlockSpec((None, 1, GW, SQ), lambda g: (layer, g, 0, 0))],
        out_specs=[pl.BlockSpec((1, xg.shape[1], GW), lambda g: (g, 0, 0)) for xg in xgs],
        out_shape=[jax.ShapeDtypeStruct((n_g, xg.shape[1], GW), F32) for xg in xgs],
        scratch_shapes=[pltpu.VMEM((GW, GW), BF16),
                        pltpu.VMEM((rows, GW), BF16),
                        pltpu.VMEM((rows, GW), F32)],
        compiler_params=_cparams(("parallel",)),
    )(*xgs, *hprevs, ktab, wout)


def _rope_tables(seq):
    inv_freq = ROPE_THETA ** (-jnp.arange(0, HEAD_DIM, 2, dtype=F32) / HEAD_DIM)
    ang = jnp.arange(seq, dtype=F32)[:, None] * inv_freq[None, :]
    cos, sin = jnp.cos(ang), jnp.sin(ang)
    return jnp.concatenate([cos, cos], axis=-1), jnp.concatenate([-sin, sin], axis=-1)


def kernel(x_prompt, x_sample, mem_prompt, mem_sample, ssm_w_in, ssm_lam_re, ssm_lam_im, ssm_log_dt, ssm_b_re,
           ssm_b_im, ssm_c_re, ssm_c_im, ssm_d, ssm_w_glu, attn_w_in, attn_sink, w_mem_kv, w_out, ln1_g, ln1_b,
           w_ff1, w_ff2, ln2_g, ln2_b):
    depth = w_out.shape[0]
    d_model = x_prompt.shape[-1]
    alpha = (2 * depth) ** 0.25
    mix_w = ssm_d.shape[-1]
    n_groups = mix_w // SSM_GROUP
    mem_w = N_MEM_HEADS * HEAD_DIM

    ssm_w_in_b = ssm_w_in.astype(BF16)
    ssm_w_glu_b = ssm_w_glu.astype(BF16)
    attn_w_in_b = attn_w_in.astype(BF16)
    w_mem_kv_b = w_mem_kv.astype(BF16)
    w_out_b = w_out.astype(BF16)
    w_ff1_b = w_ff1.astype(BF16)
    w_ff2_b = w_ff2.astype(BF16)

    ktab, win, wout, atab = _ssm_prep(ssm_lam_re, ssm_lam_im, ssm_log_dt, ssm_b_re, ssm_b_im,
                                      ssm_c_re, ssm_c_im, ssm_d.reshape(-1, n_groups, SSM_GROUP))
    shapes = [x_prompt.shape, x_sample.shape]
    ropes = [_rope_tables(s[1]) for s in shapes]
    xs = [x_prompt.reshape(-1, d_model), x_sample.reshape(-1, d_model)]
    mems = [mem_prompt, mem_sample]

    ln1_g3, ln1_b3, ln2_g3, ln2_b3 = [a.reshape(depth, 1, d_model) for a in (ln1_g, ln1_b, ln2_g, ln2_b)]

    for i in range(depth):
        j = i // 2
        kvs = [_rowmm(m.reshape(-1, d_model), w_mem_kv_b, i).reshape(m.shape[0], m.shape[1], -1) for m in mems]
        if i % 2 == 0:
            xgs, qms = zip(*[_ssm_inproj(x.reshape(s), ssm_w_in_b, j, mix_w) for x, s in zip(xs, shapes)])
            ss = _ssm_state(xgs, win, j)
            hprevs = [_ssm_scan(s.reshape(n_groups, -1, shp[0], SQ), atab, j).reshape(n_groups, -1, SQ)
                      for s, shp in zip(ss, shapes)]
            zs = _ssm_out(xgs, hprevs, ktab, wout, j)
            ymix = [_glu(z, ssm_w_glu_b, j, shp[0]) for z, shp in zip(zs, shapes)]
            ymem = [_memattn(q, 0, kv) for q, kv in zip(qms, kvs)]
        else:
            projs = [_attn_inproj(x, attn_w_in_b, j, cos2, sin2, s[1]).reshape(s[0], s[1], -1)
                     for x, s, (cos2, sin2) in zip(xs, shapes, ropes)]
            ymix = [_wattn(p, attn_sink[j]) for p in projs]
            qcol = (mix_w + 2 * N_KV_HEADS * HEAD_DIM) // mem_w
            ymem = [_memattn(p, qcol, kv) for p, kv in zip(projs, kvs)]
        xs = [_oproj(ym.reshape(-1, mix_w), ye.reshape(-1, mem_w), w_out_b, x, ln1_g3, ln1_b3, i, alpha)
              for ym, ye, x in zip(ymix, ymem, xs)]
        xs = [_ffn(x, w_ff1_b, w_ff2_b, ln2_g3, ln2_b3, i, alpha) for x in xs]

    return (xs[0].reshape(shapes[0]), xs[1].reshape(shapes[1]))
```

```python
import functools
import math

import jax
import jax.numpy as jnp
from jax import lax
from jax.experimental import pallas as pl
from jax.experimental.pallas import tpu as pltpu

F32 = jnp.float32
BF16 = jnp.bfloat16

HEAD_DIM = 128
N_KV_HEADS = 4
GQA_GROUP = 3
WINDOW = 128
N_MEM_HEADS = 4
SSM_GROUP = 16
SSM_STATE = 64
SSM_CHUNK = 128
ROPE_THETA = 10000.0
LN_EPS = 1e-5
NEG_INF = -1e30
SQ = 2 * SSM_STATE * 2
GW = SSM_GROUP * SSM_CHUNK
V7X_VMEM_LIMIT_MB = 56
V7X_VMEM_LIMIT_FFN_MB = 60


def _cparams(sem, vmem_mb=V7X_VMEM_LIMIT_MB):
    return pltpu.CompilerParams(dimension_semantics=sem, vmem_limit_bytes=vmem_mb * 1024 * 1024)


def _tile(n, target):
    if n <= target:
        return n
    for t in range(target, 7, -1):
        if n % t == 0 and t % 8 == 0:
            return t
    return n


def _layer_norm(y, g, b):
    mu = jnp.mean(y, axis=-1, keepdims=True)
    d = y - mu
    var = jnp.mean(d * d, axis=-1, keepdims=True)
    return d * lax.rsqrt(var + LN_EPS) * g + b


def _rowmm_kernel(x_ref, w_ref, o_ref, *, n_chunk):
    xb = x_ref[...].astype(BF16)
    n = w_ref.shape[1]
    for n0 in range(0, n, n_chunk):
        o_ref[:, n0:n0 + n_chunk] = jnp.dot(
            xb, w_ref[:, n0:n0 + n_chunk], preferred_element_type=F32).astype(o_ref.dtype)


def _rowmm(x, w, layer, tm_target=512):
    m, k = x.shape
    n = w.shape[2]
    tm = _tile(m, tm_target)
    n_chunk = 512 if n % 512 == 0 else n
    return pl.pallas_call(
        functools.partial(_rowmm_kernel, n_chunk=n_chunk),
        grid=(m // tm,),
        in_specs=[pl.BlockSpec((tm, k), lambda i: (i, 0)),
                  pl.BlockSpec((None, k, n), lambda i: (layer, 0, 0), pipeline_mode=pl.Buffered(1))],
        out_specs=pl.BlockSpec((tm, n), lambda i: (i, 0)),
        out_shape=jax.ShapeDtypeStruct((m, n), BF16),
        compiler_params=_cparams(("parallel",)),
    )(x, w)


RELAYOUT_ROWS = 8
CH_PASS = 512
SLAB_PITCH = CH_PASS + 8


def _chunks_per_step(bsz):
    assert RELAYOUT_ROWS % bsz == 0, "batch must divide the 8-row relayout tile"
    return RELAYOUT_ROWS // bsz


def _ssm_inproj_kernel(x_ref, wu_ref, wq_ref, xg_ref, q_ref, tt_ref, *, bsz, cpb):
    d = x_ref.shape[-1]
    xb = x_ref[...].reshape(RELAYOUT_ROWS * SSM_CHUNK, d).astype(BF16)
    q = jnp.dot(xb, wq_ref[...], preferred_element_type=F32)
    q_ref[...] = q.reshape(q_ref.shape).astype(q_ref.dtype)
    for n in range(wu_ref.shape[1] // CH_PASS):
        u = jnp.dot(xb, wu_ref[:, n * CH_PASS:(n + 1) * CH_PASS], preferred_element_type=F32)
        for b in range(bsz):
            for cc in range(cpb):
                src = (b * cpb + cc) * SSM_CHUNK
                slot = cc * bsz + b
                tt_ref[slot * SLAB_PITCH:slot * SLAB_PITCH + CH_PASS, :] = u[src:src + SSM_CHUNK, :].T
        for r in range(CH_PASS):
            g = (n * CH_PASS + r) // SSM_GROUP
            c = r % SSM_GROUP
            xg_ref[g, :, c * SSM_CHUNK:(c + 1) * SSM_CHUNK] = tt_ref[pl.ds(r, RELAYOUT_ROWS, stride=SLAB_PITCH), :]


def _ssm_inproj(x, w_in, layer, mix_w):
    bsz, seq, d = x.shape
    nc = seq // SSM_CHUNK
    cpb = _chunks_per_step(bsz)
    assert nc % cpb == 0 and mix_w % CH_PASS == 0
    n_g = mix_w // SSM_GROUP
    mem_w = w_in.shape[2] - mix_w
    x4 = x.reshape(bsz, nc, SSM_CHUNK, d)
    return pl.pallas_call(
        functools.partial(_ssm_inproj_kernel, bsz=bsz, cpb=cpb),
        grid=(nc // cpb,),
        in_specs=[pl.BlockSpec((bsz, cpb, SSM_CHUNK, d), lambda i: (0, i, 0, 0)),
                  pl.BlockSpec((None, d, mix_w), lambda i: (layer, 0, 0), pipeline_mode=pl.Buffered(1)),
                  pl.BlockSpec((None, d, mem_w), lambda i: (layer, 0, mix_w // mem_w),
                               pipeline_mode=pl.Buffered(1))],
        out_specs=[pl.BlockSpec((n_g, RELAYOUT_ROWS, GW), lambda i: (0, i, 0)),
                   pl.BlockSpec((bsz, cpb * SSM_CHUNK, mem_w), lambda i: (0, i, 0))],
        out_shape=[jax.ShapeDtypeStruct((n_g, nc * bsz, GW), F32),
                   jax.ShapeDtypeStruct((bsz, seq, mem_w), BF16)],
        scratch_shapes=[pltpu.VMEM((RELAYOUT_ROWS * SLAB_PITCH, SSM_CHUNK), F32)],
        compiler_params=_cparams(("parallel",)),
    )(x4, w_in, w_in)


def _glu_kernel(z_ref, wa_ref, wg_ref, o_ref, tt_ref, zt_ref, *, bsz, cpb):
    n_out = wa_ref.shape[1]
    for n in range(zt_ref.shape[1] // CH_PASS):
        for r in range(CH_PASS):
            g = (n * CH_PASS + r) // SSM_GROUP
            c = r % SSM_GROUP
            tt_ref[pl.ds(r, RELAYOUT_ROWS, stride=SLAB_PITCH), :] = z_ref[g, :, c * SSM_CHUNK:(c + 1) * SSM_CHUNK]
        for b in range(bsz):
            for cc in range(cpb):
                dst = (b * cpb + cc) * SSM_CHUNK
                slot = cc * bsz + b
                zt_ref[dst:dst + SSM_CHUNK, n * CH_PASS:(n + 1) * CH_PASS] = (
                    tt_ref[slot * SLAB_PITCH:slot * SLAB_PITCH + CH_PASS, :].T.astype(zt_ref.dtype))
    z = zt_ref[...]
    for n0 in range(0, n_out, CH_PASS):
        a = jnp.dot(z, wa_ref[:, n0:n0 + CH_PASS], preferred_element_type=F32)
        g = jnp.dot(z, wg_ref[:, n0:n0 + CH_PASS], preferred_element_type=F32)
        y = (a * jax.nn.sigmoid(g)).astype(o_ref.dtype)
        o_ref[:, :, n0:n0 + CH_PASS] = y.reshape(bsz, cpb * SSM_CHUNK, CH_PASS)


def _glu(z, w_glu, layer, bsz):
    n_g, rows, _ = z.shape
    mix_w = n_g * SSM_GROUP
    cpb = _chunks_per_step(bsz)
    nc = rows // bsz
    return pl.pallas_call(
        functools.partial(_glu_kernel, bsz=bsz, cpb=cpb),
        grid=(nc // cpb,),
        in_specs=[pl.BlockSpec((n_g, RELAYOUT_ROWS, GW), lambda i: (0, i, 0)),
                  pl.BlockSpec((None, mix_w, mix_w), lambda i: (layer, 0, 0), pipeline_mode=pl.Buffered(1)),
                  pl.BlockSpec((None, mix_w, mix_w), lambda i: (layer, 0, 1), pipeline_mode=pl.Buffered(1))],
        out_specs=pl.BlockSpec((bsz, cpb * SSM_CHUNK, mix_w), lambda i: (0, i, 0)),
        out_shape=jax.ShapeDtypeStruct((bsz, nc * SSM_CHUNK, mix_w), BF16),
        scratch_shapes=[pltpu.VMEM((RELAYOUT_ROWS * SLAB_PITCH, SSM_CHUNK), F32),
                        pltpu.VMEM((RELAYOUT_ROWS * SSM_CHUNK, mix_w), BF16)],
        compiler_params=_cparams(("parallel",)),
    )(z, w_glu, w_glu)


def _oproj_kernel(ymix_ref, ymem_ref, wa_ref, wb_ref, x_ref, g_ref, b_ref, o_ref, *, alpha, n_sub):
    sub = x_ref.shape[0] // n_sub
    for r in range(n_sub):
        rs = slice(r * sub, (r + 1) * sub)
        acc = jnp.dot(ymix_ref[rs, :], wa_ref[...], preferred_element_type=F32)
        acc = acc + jnp.dot(ymem_ref[rs, :], wb_ref[...], preferred_element_type=F32)
        y = alpha * x_ref[rs, :] + acc
        o_ref[rs, :] = _layer_norm(y, g_ref[...], b_ref[...])


def _oproj(ymix, ymem, w_out, x, g, b, layer, alpha, tm_target=1024, n_sub=4):
    m, d = x.shape
    kmix = ymix.shape[1]
    kmem = ymem.shape[1]
    tm = _tile(m, tm_target)
    if tm % (8 * n_sub):
        n_sub = 1
    return pl.pallas_call(
        functools.partial(_oproj_kernel, alpha=alpha, n_sub=n_sub),
        grid=(m // tm,),
        in_specs=[pl.BlockSpec((tm, kmix), lambda i: (i, 0)),
                  pl.BlockSpec((tm, kmem), lambda i: (i, 0)),
                  pl.BlockSpec((None, kmix, d), lambda i: (layer, 0, 0), pipeline_mode=pl.Buffered(1)),
                  pl.BlockSpec((None, kmem, d), lambda i: (layer, kmix // kmem, 0), pipeline_mode=pl.Buffered(1)),
                  pl.BlockSpec((tm, d), lambda i: (i, 0)),
                  pl.BlockSpec((None, 1, d), lambda i: (layer, 0, 0)),
                  pl.BlockSpec((None, 1, d), lambda i: (layer, 0, 0))],
        out_specs=pl.BlockSpec((tm, d), lambda i: (i, 0)),
        out_shape=jax.ShapeDtypeStruct((m, d), F32),
        compiler_params=_cparams(("parallel",)),
    )(ymix, ymem, w_out, w_out, x, g, b)


def _ffn_kernel(x_ref, w1_ref, w2_ref, g_ref, b_ref, o_ref, xb_ref, *, alpha, n_sub, f_chunk):
    f = pl.program_id(1)
    last = pl.num_programs(1) - 1
    chunks = [slice(c, c + f_chunk) for c in range(0, w1_ref.shape[1], f_chunk)]

    def ffn_part(xb, cs):
        h = jnp.dot(xb, w1_ref[:, cs], preferred_element_type=F32)
        h = jnp.square(jnp.maximum(h, 0.0)).astype(BF16)
        return jnp.dot(h, w2_ref[cs, :], preferred_element_type=F32)

    @pl.when(f == 0)
    def _():
        xb = x_ref[...].astype(BF16)
        xb_ref[...] = xb
        o_ref[...] = alpha * x_ref[...] + ffn_part(xb, chunks[0])
        for cs in chunks[1:]:
            o_ref[...] += ffn_part(xb, cs)

    @pl.when(jnp.logical_and(f > 0, f < last))
    def _():
        for cs in chunks:
            o_ref[...] += ffn_part(xb_ref[...], cs)

    @pl.when(f == last)
    def _():
        for cs in chunks[:-1]:
            o_ref[...] += ffn_part(xb_ref[...], cs)
        sub = o_ref.shape[0] // n_sub
        for r in range(n_sub):
            rows = slice(r * sub, (r + 1) * sub)
            y = o_ref[rows, :] + ffn_part(xb_ref[rows, :], chunks[-1])
            o_ref[rows, :] = _layer_norm(y, g_ref[...], b_ref[...])


def _ffn(x, w1, w2, g, b, layer, alpha, tm_target=512, tf_target=2048, f_chunk=1024, n_sub=2):
    m, d = x.shape
    dff = w1.shape[2]
    tm = _tile(m, tm_target)
    tf = _tile(dff, tf_target)
    assert dff // tf >= 2, "the kernel has distinct first and last d_ff slices"
    return pl.pallas_call(
        functools.partial(_ffn_kernel, alpha=alpha, n_sub=n_sub if tm % (16 * n_sub) == 0 else 1,
                          f_chunk=f_chunk if tf % f_chunk == 0 else tf),
        grid=(m // tm, dff // tf),
        in_specs=[pl.BlockSpec((tm, d), lambda i, f: (i, 0)),
                  pl.BlockSpec((None, d, tf), lambda i, f: (layer, 0, f)),
                  pl.BlockSpec((None, tf, d), lambda i, f: (layer, f, 0)),
                  pl.BlockSpec((None, 1, d), lambda i, f: (layer, 0, 0)),
                  pl.BlockSpec((None, 1, d), lambda i, f: (layer, 0, 0))],
        out_specs=pl.BlockSpec((tm, d), lambda i, f: (i, 0)),
        out_shape=jax.ShapeDtypeStruct((m, d), F32),
        scratch_shapes=[pltpu.VMEM((tm, d), BF16)],
        compiler_params=_cparams(("parallel", "arbitrary"), vmem_mb=V7X_VMEM_LIMIT_FFN_MB),
    )(x, w1, w2, g, b)


def _memattn_kernel(q_ref, k_ref, v_ref, o_ref):
    scale = HEAD_DIM ** -0.5
    for h in range(N_MEM_HEADS):
        sl = slice(h * HEAD_DIM, (h + 1) * HEAD_DIM)
        s = lax.dot_general(q_ref[0, :, sl], k_ref[0, :, sl], (((1,), (1,)), ((), ())),
                            preferred_element_type=F32) * scale
        m = jnp.max(s, axis=-1, keepdims=True)
        p = jnp.exp(s - m)
        l = jnp.sum(p, axis=-1, keepdims=True)
        o = jnp.dot(p.astype(BF16), v_ref[0, :, sl], preferred_element_type=F32)
        o_ref[0, :, sl] = (o / l).astype(o_ref.dtype)


def _memattn(proj, q_col_block, kv, tq_target=1024):
    bsz, seq, _ = proj.shape
    n_mem = kv.shape[1]
    w = N_MEM_HEADS * HEAD_DIM
    tq = _tile(seq, tq_target)
    return pl.pallas_call(
        _memattn_kernel,
        grid=(bsz, seq // tq),
        in_specs=[pl.BlockSpec((1, tq, w), lambda b, i: (b, i, q_col_block)),
                  pl.BlockSpec((1, n_mem, w), lambda b, i: (b, 0, 0)),
                  pl.BlockSpec((1, n_mem, w), lambda b, i: (b, 0, 1))],
        out_specs=pl.BlockSpec((1, tq, w), lambda b, i: (b, i, 0)),
        out_shape=jax.ShapeDtypeStruct((bsz, seq, w), BF16),
        compiler_params=_cparams(("parallel", "parallel")),
    )(proj, kv, kv)


def _rope(x, cos2, sin2):
    return x * cos2 + pltpu.roll(x, HEAD_DIM // 2, axis=1) * sin2


def _wattn_kernel(sink_ref, q_ref, kp_ref, kc_ref, kn_ref, vp_ref, vc_ref, vn_ref,
                  o_ref, kr_ref, vr_ref, *, tq, seq):
    i = pl.program_id(1)
    base = i * tq
    blk = WINDOW
    nqb = tq // blk

    kr_ref[0:blk, :] = kp_ref[0]
    kr_ref[blk:blk + tq, :] = kc_ref[0]
    kr_ref[blk + tq:, :] = kn_ref[0]
    ones = jnp.ones((tq + 2 * blk, HEAD_DIM), BF16)
    for h in range(N_KV_HEADS):
        src = slice(h * HEAD_DIM, (h + 1) * HEAD_DIM)
        dst = slice(2 * h * HEAD_DIM, (2 * h + 1) * HEAD_DIM)
        vr_ref[0:blk, dst] = vp_ref[0, :, src]
        vr_ref[blk:blk + tq, dst] = vc_ref[0, :, src]
        vr_ref[blk + tq:, dst] = vn_ref[0, :, src]
        vr_ref[:, (2 * h + 1) * HEAD_DIM:(2 * h + 2) * HEAD_DIM] = ones

    r = lax.broadcasted_iota(jnp.int32, (blk, 3 * blk), 0)
    c = lax.broadcasted_iota(jnp.int32, (blk, 3 * blk), 1)
    rel = c - blk - r
    band = jnp.abs(rel) <= WINDOW

    for jb in range(nqb):
        kpos = base + (jb - 1) * blk + c
        valid = band & (kpos >= 0) & (kpos < seq)
        bias = jnp.where(valid, 0.0, NEG_INF).astype(F32)
        for hk in range(N_KV_HEADS):
            ksl = slice(hk * HEAD_DIM, (hk + 1) * HEAD_DIM)
            kw = kr_ref[jb * blk:(jb + 3) * blk, ksl]
            vw = vr_ref[jb * blk:(jb + 3) * blk, 2 * hk * HEAD_DIM:(2 * hk + 2) * HEAD_DIM]
            qs = jnp.concatenate(
                [q_ref[0, jb * blk:(jb + 1) * blk, (hk * GQA_GROUP + g) * HEAD_DIM:(hk * GQA_GROUP + g + 1) * HEAD_DIM]
                 for g in range(GQA_GROUP)], axis=0)
            s = lax.dot_general(qs, kw, (((1,), (1,)), ((), ())), preferred_element_type=F32)
            ps = []
            sink_terms = []
            for g in range(GQA_GROUP):
                sk = sink_ref[hk * GQA_GROUP + g]
                sg = s[g * blk:(g + 1) * blk, :] + bias
                m = jnp.maximum(jnp.max(sg, axis=-1, keepdims=True), sk)
                ps.append(jnp.exp(sg - m).astype(BF16))
                sink_terms.append(jnp.exp(sk - m))
            pcat = jnp.concatenate(ps, axis=0)
            o = jnp.dot(pcat, vw, preferred_element_type=F32)
            for g in range(GQA_GROUP):
                hq = hk * GQA_GROUP + g
                den = o[g * blk:(g + 1) * blk, HEAD_DIM:] + sink_terms[g]
                o_ref[0, jb * blk:(jb + 1) * blk, hq * HEAD_DIM:(hq + 1) * HEAD_DIM] = (
                    o[g * blk:(g + 1) * blk, :HEAD_DIM] / den).astype(o_ref.dtype)


def _attn_inproj_kernel(x_ref, w_ref, cos_ref, sin_ref, o_ref, *, n_q_heads, n_rope_heads, n_chunk):
    xb = x_ref[...].astype(BF16)
    cos2 = cos_ref[...]
    sin2 = sin_ref[...]
    scale = HEAD_DIM ** -0.5
    for n0 in range(0, w_ref.shape[1], n_chunk):
        y = jnp.dot(xb, w_ref[:, n0:n0 + n_chunk], preferred_element_type=F32)
        for c0 in range(0, n_chunk, HEAD_DIM):
            head = (n0 + c0) // HEAD_DIM
            blk = y[:, c0:c0 + HEAD_DIM]
            if head < n_rope_heads:
                blk = _rope(blk, cos2, sin2)
            if head < n_q_heads:
                blk = blk * scale
            o_ref[:, n0 + c0:n0 + c0 + HEAD_DIM] = blk.astype(o_ref.dtype)


def _attn_inproj(x, w, layer, cos2, sin2, seq, tm_target=512):
    m, k = x.shape
    n = w.shape[2]
    tm = _tile(seq, tm_target)
    n_q_heads = N_KV_HEADS * GQA_GROUP
    return pl.pallas_call(
        functools.partial(_attn_inproj_kernel, n_q_heads=n_q_heads, n_rope_heads=n_q_heads + N_KV_HEADS,
                          n_chunk=4 * HEAD_DIM),
        grid=(m // tm,),
        in_specs=[pl.BlockSpec((tm, k), lambda i: (i, 0)),
                  pl.BlockSpec((None, k, n), lambda i: (layer, 0, 0), pipeline_mode=pl.Buffered(1)),
                  pl.BlockSpec((tm, HEAD_DIM), lambda i: (i % (seq // tm), 0)),
                  pl.BlockSpec((tm, HEAD_DIM), lambda i: (i % (seq // tm), 0))],
        out_specs=pl.BlockSpec((tm, n), lambda i: (i, 0)),
        out_shape=jax.ShapeDtypeStruct((m, n), BF16),
        compiler_params=_cparams(("parallel",)),
    )(x, w, cos2, sin2)


def _wattn(proj, sink, tq_target=512):
    bsz, seq, _ = proj.shape
    blk = WINDOW
    qw = N_KV_HEADS * GQA_GROUP * HEAD_DIM
    kvw = N_KV_HEADS * HEAD_DIM
    tq = _tile(seq, tq_target)
    r = tq // blk
    nb = seq // blk
    kcol = qw // kvw
    vcol = kcol + 1
    in_specs = [
        pl.BlockSpec(memory_space=pltpu.SMEM),
        pl.BlockSpec((1, tq, qw), lambda b, i: (b, i, 0)),
        pl.BlockSpec((1, blk, kvw), lambda b, i: (b, jnp.maximum(i * r - 1, 0), kcol)),
        pl.BlockSpec((1, tq, kvw), lambda b, i: (b, i, kcol)),
        pl.BlockSpec((1, blk, kvw), lambda b, i: (b, jnp.minimum(i * r + r, nb - 1), kcol)),
        pl.BlockSpec((1, blk, kvw), lambda b, i: (b, jnp.maximum(i * r - 1, 0), vcol)),
        pl.BlockSpec((1, tq, kvw), lambda b, i: (b, i, vcol)),
        pl.BlockSpec((1, blk, kvw), lambda b, i: (b, jnp.minimum(i * r + r, nb - 1), vcol)),
    ]
    return pl.pallas_call(
        functools.partial(_wattn_kernel, tq=tq, seq=seq),
        grid=(bsz, seq // tq),
        in_specs=in_specs,
        out_specs=pl.BlockSpec((1, tq, qw), lambda b, i: (b, i, 0)),
        out_shape=jax.ShapeDtypeStruct((bsz, seq, qw), BF16),
        scratch_shapes=[pltpu.VMEM((tq + 2 * blk, kvw), BF16),
                        pltpu.VMEM((tq + 2 * blk, 2 * kvw), BF16)],
        compiler_params=_cparams(("parallel", "parallel")),
    )(sink, proj, proj, proj, proj, proj, proj, proj)


def _cpow(e, re, im):
    mg = jnp.exp(e * re)
    return mg * jnp.cos(e * im), mg * jnp.sin(e * im)


def _pow_rows(hi, lo):
    (hi_re, hi_im), (lo_re, lo_im) = hi, lo
    rows_re, rows_im = [], []
    for a in range(hi_re.shape[0]):
        ar = hi_re[a:a + 1, :]
        ai = hi_im[a:a + 1, :]
        rows_re.append(ar * lo_re - ai * lo_im)
        rows_im.append(ar * lo_im + ai * lo_re)
    return jnp.concatenate(rows_re, axis=0), jnp.concatenate(rows_im, axis=0)


def _ssm_prep_kernel(rowp_ref, blre_ref, blim_ref, clre_ref, clim_ref, dcol_ref,
                     ktab_ref, win_ref, wout_ref, atab_ref):
    t_len = SSM_CHUNK
    half = SQ // 2
    lane = lax.broadcasted_iota(jnp.int32, (1, SQ), 1)
    is_re_l = (lane % half) < SSM_STATE
    is_f_l = lane < half

    lam_re = rowp_ref[0, 0, 0:1, :]
    lam_im = rowp_ref[0, 0, 1:2, :]
    dt = jnp.exp(rowp_ref[0, 0, 2:3, :])
    re = lam_re * dt
    im = lam_im * dt
    mag = jnp.exp(re)
    n_re = mag * jnp.cos(im) - 1.0
    n_im = mag * jnp.sin(im)
    den = lam_re * lam_re + lam_im * lam_im
    co_re = (n_re * lam_re + n_im * lam_im) / den
    co_im = (n_im * lam_re - n_re * lam_im) / den
    b_re = blre_ref[0, 0]
    b_im = blim_ref[0, 0]
    bb_re = co_re * b_re - co_im * b_im
    bb_im = co_re * b_im + co_im * b_re

    a_i = lax.broadcasted_iota(jnp.int32, (t_len // 8, SQ), 0)
    i_i = lax.broadcasted_iota(jnp.int32, (8, SQ), 0)
    up, dn = 8 * a_i, t_len - 8 - 8 * a_i
    hi_dn_up = _cpow(jnp.where(is_f_l, dn, up).astype(F32), re, im)
    hi_up_dn = _cpow(jnp.where(is_f_l, up, dn).astype(F32), re, im)
    l_re, l_im = _pow_rows(hi_dn_up, _cpow(jnp.where(is_f_l, 7 - i_i, i_i).astype(F32), re, im))
    p1 = jnp.where(is_re_l, bb_re, bb_im)
    p2 = jnp.where(is_re_l, -bb_im, bb_re)
    for c in range(SSM_GROUP):
        win_ref[0, 0, c * t_len:(c + 1) * t_len, :] = (
            l_re * p1[c:c + 1, :] + l_im * p2[c:c + 1, :]).astype(win_ref.dtype)

    mg_t = jnp.exp(t_len * re)
    a_re = mg_t * jnp.cos(t_len * im)
    a_im = mg_t * jnp.sin(t_len * im)
    atab_ref[0, 0, 0:1, :] = a_re
    atab_ref[0, 0, 1:2, :] = jnp.where(is_re_l, -a_im, a_im)
    atab_ref[0, 0, 2:8, :] = jnp.zeros((6, SQ), F32)

    c_re = clre_ref[0, 0]
    c_im = clim_ref[0, 0]
    blocks = []
    for ci in range(SSM_GROUP):
        br = bb_re[ci:ci + 1, :]
        bi = bb_im[ci:ci + 1, :]
        d_re = c_re * br - c_im * bi
        d_im = c_re * bi + c_im * br
        blocks.append(jnp.where(is_re_l, d_re, -d_im))
    dmat = jnp.concatenate(blocks, axis=0)

    t2_re, t2_im = _pow_rows(hi_up_dn, _cpow(jnp.where(is_f_l, i_i, 8 - i_i).astype(F32), re, im))
    k_row = lax.broadcasted_iota(jnp.int32, (t_len, SQ), 0)
    lt = jnp.where(is_re_l, t2_re, t2_im)
    lt = jnp.where(jnp.logical_and(jnp.logical_not(is_f_l), k_row == 0), 0.0, lt)

    def taps_of(lanes):
        return lax.dot_general(dmat[:, lanes], lt[:, lanes], (((1,), (1,)), ((), ())),
                               preferred_element_type=F32, precision=lax.Precision.HIGHEST)

    k_f = taps_of(slice(0, half))
    k_b = taps_of(slice(half, SQ))
    in_b_re = jnp.logical_and(lane >= half, lane < half + SSM_STATE)
    kb0 = jnp.sum(jnp.where(in_b_re, dmat, 0.0), axis=1, keepdims=True)
    tap0 = lax.broadcasted_iota(jnp.int32, (SSM_GROUP * SSM_GROUP, t_len), 1) == 0
    k_f = k_f + jnp.where(tap0, kb0 + dcol_ref[0, 0], 0.0)
    taps = jnp.concatenate([k_f, k_b], axis=1).astype(BF16).astype(F32)
    bits = pltpu.bitcast(taps, jnp.int32)
    ktab_ref[0, 0] = pltpu.roll(bits, 1, axis=1) | lax.shift_right_logical(bits, 16)

    t3_re, t3_im = _pow_rows(hi_up_dn, _cpow(jnp.where(is_f_l, i_i + 1, 8 - i_i).astype(F32), re, im))
    w_cr = jnp.where(is_re_l, t3_re, -t3_im)
    w_ci = jnp.where(is_re_l, -t3_im, -t3_re)
    for c in range(SSM_GROUP):
        wout_ref[0, 0, c * t_len:(c + 1) * t_len, :] = (
            c_re[c:c + 1, :] * w_cr + c_im[c:c + 1, :] * w_ci).astype(wout_ref.dtype)


def _ssm_prep(lam_re, lam_im, log_dt, b_re, b_im, c_re, c_im, d_skip):
    ns, _, n_g, n_p = lam_re.shape
    cg = SSM_GROUP

    def lanes(a):
        return jnp.concatenate([a[:, 0], a[:, 0], a[:, 1], a[:, 1]], axis=-1)

    ldt = jnp.broadcast_to(log_dt[..., None], lam_re.shape)
    rowp = jnp.stack([lanes(lam_re), lanes(lam_im), lanes(ldt)] + [jnp.zeros((ns, n_g, SQ), F32)] * 5, axis=2)
    bl_re = lanes(jnp.swapaxes(b_re, -1, -2))
    bl_im = lanes(jnp.swapaxes(b_im, -1, -2))
    cl_re = lanes(c_re)
    cl_im = lanes(c_im)
    dcol =(jnp.eye(cg, dtype=F32)[None, None] * d_skip.reshape(ns, n_g, 1, cg)).reshape(ns, n_g, cg * cg, 1)

    def spec(*shape):
        nd = len(shape)
        return pl.BlockSpec((1, 1) + shape, lambda j, g: (j, g) + (0,) * nd)

    return pl.pallas_call(
        _ssm_prep_kernel,
        grid=(ns, n_g),
        in_specs=[spec(8, SQ), spec(cg, SQ), spec(cg, SQ), spec(cg, SQ), spec(cg, SQ),
                  spec(cg * cg, 1)],
        out_specs=[spec(cg * cg, 2 * SSM_CHUNK), spec(GW, SQ), spec(GW, SQ), spec(8, SQ)],
        out_shape=[jax.ShapeDtypeStruct((ns, n_g, cg * cg, 2 * SSM_CHUNK), jnp.int32),
                   jax.ShapeDtypeStruct((ns, n_g, GW, SQ), BF16),
                   jax.ShapeDtypeStruct((ns, n_g, GW, SQ), BF16),
                   jax.ShapeDtypeStruct((ns, n_g, 8, SQ), F32)],
        compiler_params=_cparams(("parallel", "parallel")),
    )(rowp, bl_re, bl_im, cl_re, cl_im, dcol)


def _ssm_state_kernel(*refs, n_in):
    win_ref = refs[n_in]
    for x_ref, s_ref in zip(refs[:n_in], refs[n_in + 1:]):
        s_ref[0] = jnp.dot(x_ref[0].astype(BF16), win_ref[0], preferred_element_type=F32)


def _ssm_state(xgs, win, layer):
    n_g = win.shape[1]
    n_in = len(xgs)
    return pl.pallas_call(
        functools.partial(_ssm_state_kernel, n_in=n_in),
        grid=(n_g,),
        in_specs=[pl.BlockSpec((1, xg.shape[1], GW), lambda g: (g, 0, 0)) for xg in xgs]
        + [pl.BlockSpec((None, 1, GW, SQ), lambda g: (layer, g, 0, 0))],
        out_specs=[pl.BlockSpec((1, xg.shape[1], SQ), lambda g: (g, 0, 0)) for xg in xgs],
        out_shape=[jax.ShapeDtypeStruct((n_g, xg.shape[1], SQ), F32) for xg in xgs],
        compiler_params=_cparams(("parallel",)),
    )(*xgs, win)


SCAN_GROUPS = 8


def _ssm_scan_kernel(s_ref, a_ref, h_ref, *, n_chunks):
    half = SQ // 2
    nb = s_ref.shape[2]
    for k in range(s_ref.shape[0]):
        a1_f = a_ref[k, 0:1, 0:half]
        a2_f = a_ref[k, 1:2, 0:half]
        a1_b = a_ref[k, 0:1, half:]
        a2_b = a_ref[k, 1:2, half:]
        hf = jnp.zeros((nb, half), F32)
        hb = jnp.zeros((nb, half), F32)
        for i in range(n_chunks):
            j = n_chunks - 1 - i
            h_ref[k, i, :, 0:half] = hf.astype(h_ref.dtype)
            h_ref[k, j, :, half:] = hb.astype(h_ref.dtype)
            hf = a1_f * hf + a2_f * pltpu.roll(hf, SSM_STATE, axis=1) + s_ref[k, i, :, 0:half]
            hb = a1_b * hb + a2_b * pltpu.roll(hb, SSM_STATE, axis=1) + s_ref[k, j, :, half:]


def _ssm_scan(s, atab, layer):
    n_g, n_chunks, nb, _ = s.shape
    gb = SCAN_GROUPS if n_g % SCAN_GROUPS == 0 else 1
    return pl.pallas_call(
        functools.partial(_ssm_scan_kernel, n_chunks=n_chunks),
        grid=(n_g // gb,),
        in_specs=[pl.BlockSpec((gb, n_chunks, nb, SQ), lambda g: (g, 0, 0, 0)),
                  pl.BlockSpec((None, gb, 8, SQ), lambda g: (layer, g, 0, 0))],
        out_specs=pl.BlockSpec((gb, n_chunks, nb, SQ), lambda g: (g, 0, 0, 0)),
        out_shape=jax.ShapeDtypeStruct((n_g, n_chunks, nb, SQ), BF16),
        compiler_params=_cparams(("parallel",)),
    )(s, atab)


def _gelu_tanh(y):
    return 0.5 * y * (1.0 + jnp.tanh(math.sqrt(2.0 / math.pi) * (y + 0.044715 * (y * y * y))))


SSM_OUT_KSLICE = 256


def _ssm_out_kernel(*refs, n_in):
    x_refs = refs[:n_in]
    h_refs = refs[n_in:2 * n_in]
    ktab_ref, wout_ref = refs[2 * n_in:2 * n_in + 2]
    z_refs = refs[2 * n_in + 2:3 * n_in + 2]
    m_ref, xb_ref, y_ref = refs[3 * n_in + 2:]
    t_len = SSM_CHUNK
    offs = [0]
    for x_ref in x_refs:
        offs.append(offs[-1] + x_ref.shape[1])

    for x_ref, h_ref, lo, hi in zip(x_refs, h_refs, offs[:-1], offs[1:]):
        xb_ref[lo:hi, :] = x_ref[0].astype(BF16)
        y_ref[lo:hi, :] = lax.dot_general(h_ref[0], wout_ref[0], (((1,), (1,)), ((), ())),
                                          preferred_element_type=F32)

    for kk in range(GW // SSM_OUT_KSLICE):
        for ci in range(kk * SSM_OUT_KSLICE // t_len, (kk + 1) * SSM_OUT_KSLICE // t_len):
            for co in range(SSM_GROUP):
                r = ci * SSM_GROUP + co
                p8 = jnp.broadcast_to(ktab_ref[0, r:r + 1, :], (8, 2 * t_len))
                for a in range(t_len // 16):
                    w = pltpu.roll(p8, 16 * a, axis=1, stride=2, stride_axis=0)[:, :t_len]
                    m_ref[ci * t_len + 16 * a:ci * t_len + 16 * a + 16, co * t_len:(co + 1) * t_len] = (
                        pltpu.bitcast(w, m_ref.dtype))
        ks = slice(kk * SSM_OUT_KSLICE, (kk + 1) * SSM_OUT_KSLICE)
        y_ref[...] += jnp.dot(xb_ref[:, ks], m_ref[ks, :], preferred_element_type=F32)

    for z_ref, lo, hi in zip(z_refs, offs[:-1], offs[1:]):
        z_ref[0] = _gelu_tanh(y_ref[lo:hi, :]).astype(z_ref.dtype)


def _ssm_out(xgs, hprevs, ktab, wout, layer):
    n_g = ktab.shape[1]
    n_in = len(xgs)
    rows = sum(xg.shape[1] for xg in xgs)
    return pl.pallas_call(
        functools.partial(_ssm_out_kernel, n_in=n_in),
        grid=(n_g,),
        in_specs=[pl.BlockSpec((1, xg.shape[1], GW), lambda g: (g, 0, 0)) for xg in xgs]
        + [pl.BlockSpec((1, h.shape[1], SQ), lambda g: (g, 0, 0)) for h in hprevs]
        + [pl.BlockSpec((None, 1, SSM_GROUP * SSM_GROUP, 2 * SSM_CHUNK), lambda g: (layer, g, 0, 0)),
           pl.BlockSpec((None, 1, GW, SQ), lambda g: (layer, g, 0, 0))],
        out_specs=[pl.BlockSpec((1, xg.shape[1], GW), lambda g: (g, 0, 0)) for xg in xgs],
        out_shape=[jax.ShapeDtypeStruct((n_g, xg.shape[1], GW), F32) for xg in xgs],
        scratch_shapes=[pltpu.VMEM((GW, GW), BF16),
                        pltpu.VMEM((rows, GW), BF16),
                        pltpu.VMEM((rows, GW), F32)],
        compiler_params=_cparams(("parallel",)),
    )(*xgs, *hprevs, ktab, wout)


def _rope_tables(seq):
    inv_freq = ROPE_THETA ** (-jnp.arange(0, HEAD_DIM, 2, dtype=F32) / HEAD_DIM)
    ang = jnp.arange(seq, dtype=F32)[:, None] * inv_freq[None, :]
    cos, sin = jnp.cos(ang), jnp.sin(ang)
    return jnp.concatenate([cos, cos], axis=-1), jnp.concatenate([-sin, sin], axis=-1)


def kernel(x_prompt, x_sample, mem_prompt, mem_sample, ssm_w_in, ssm_lam_re, ssm_lam_im, ssm_log_dt, ssm_b_re,
           ssm_b_im, ssm_c_re, ssm_c_im, ssm_d, ssm_w_glu, attn_w_in, attn_sink, w_mem_kv, w_out, ln1_g, ln1_b,
           w_ff1, w_ff2, ln2_g, ln2_b):
    depth = w_out.shape[0]
    d_model = x_prompt.shape[-1]
    alpha = (2 * depth) ** 0.25
    mix_w = ssm_d.shape[-1]
    n_groups = mix_w // SSM_GROUP
    mem_w = N_MEM_HEADS * HEAD_DIM

    ssm_w_in_b = ssm_w_in.astype(BF16)
    ssm_w_glu_b = ssm_w_glu.astype(BF16)
    attn_w_in_b = attn_w_in.astype(BF16)
    w_mem_kv_b = w_mem_kv.astype(BF16)
    w_out_b = w_out.astype(BF16)
    w_ff1_b = w_ff1.astype(BF16)
    w_ff2_b = w_ff2.astype(BF16)

    ktab, win, wout, atab = _ssm_prep(ssm_lam_re, ssm_lam_im, ssm_log_dt, ssm_b_re, ssm_b_im,
                                      ssm_c_re, ssm_c_im, ssm_d.reshape(-1, n_groups, SSM_GROUP))
    shapes = [x_prompt.shape, x_sample.shape]
    ropes = [_rope_tables(s[1]) for s in shapes]
    xs = [x_prompt.reshape(-1, d_model), x_sample.reshape(-1, d_model)]
    mems = [mem_prompt, mem_sample]

    ln1_g3, ln1_b3, ln2_g3, ln2_b3 = [a.reshape(depth, 1, d_model) for a in (ln1_g, ln1_b, ln2_g, ln2_b)]

    for i in range(depth):
        j = i // 2
        kvs = [_rowmm(m.reshape(-1, d_model), w_mem_kv_b, i).reshape(m.shape[0], m.shape[1], -1) for m in mems]
        if i % 2 == 0:
            xgs, qms = zip(*[_ssm_inproj(x.reshape(s), ssm_w_in_b, j, mix_w) for x, s in zip(xs, shapes)])
            ss = _ssm_state(xgs, win, j)
            hprevs = [_ssm_scan(s.reshape(n_groups, -1, shp[0], SQ), atab, j).reshape(n_groups, -1, SQ)
                      for s, shp in zip(ss, shapes)]
            zs = _ssm_out(xgs, hprevs, ktab, wout, j)
            ymix = [_glu(z, ssm_w_glu_b, j, shp[0]) for z, shp in zip(zs, shapes)]
            ymem = [_memattn(q, 0, kv) for q, kv in zip(qms, kvs)]
        else:
            projs = [_attn_inproj(x, attn_w_in_b, j, cos2, sin2, s[1]).reshape(s[0], s[1], -1)
                     for x, s, (cos2, sin2) in zip(xs, shapes, ropes)]
            ymix = [_wattn(p, attn_sink[j]) for p in projs]
            qcol = (mix_w + 2 * N_KV_HEADS * HEAD_DIM) // mem_w
            ymem = [_memattn(p, qcol, kv) for p, kv in zip(projs, kvs)]
        xs = [_oproj(ym.reshape(-1, mix_w), ye.reshape(-1, mem_w), w_out_b, x, ln1_g3, ln1_b3, i, alpha)
              for ym, ye, x in zip(ymix, ymem, xs)]
        xs = [_ffn(x, w_ff1_b, w_ff2_b, ln2_g3, ln2_b3, i, alpha) for x in xs]

    return (xs[0].reshape(shapes[0]), xs[1].reshape(shapes[1]))
```

```python
import functools
import math

import jax
import jax.numpy as jnp
from jax import lax
from jax.experimental import pallas as pl
from jax.experimental.pallas import tpu as pltpu

F32 = jnp.float32
BF16 = jnp.bfloat16

HEAD_DIM = 128
N_KV_HEADS = 4
GQA_GROUP = 3
WINDOW = 128
N_MEM_HEADS = 4
SSM_GROUP = 16
SSM_STATE = 64
SSM_CHUNK = 128
ROPE_THETA = 10000.0
LN_EPS = 1e-5
NEG_INF = -1e30
SQ = 2 * SSM_STATE * 2
GW = SSM_GROUP * SSM_CHUNK
V7X_VMEM_LIMIT_MB = 56
V7X_VMEM_LIMIT_FFN_MB = 60


def _cparams(sem, vmem_mb=V7X_VMEM_LIMIT_MB):
    return pltpu.CompilerParams(dimension_semantics=sem, vmem_limit_bytes=vmem_mb * 1024 * 1024)


def _tile(n, target):
    if n <= target:
        return n
    for t in range(target, 7, -1):
        if n % t == 0 and t % 8 == 0:
            return t
    return n


def _layer_norm(y, g, b):
    mu = jnp.mean(y, axis=-1, keepdims=True)
    d = y - mu
    var = jnp.mean(d * d, axis=-1, keepdims=True)
    return d * lax.rsqrt(var + LN_EPS) * g + b


def _rowmm_kernel(x_ref, w_ref, o_ref, *, n_chunk):
    xb = x_ref[...].astype(BF16)
    n = w_ref.shape[1]
    for n0 in range(0, n, n_chunk):
        o_ref[:, n0:n0 + n_chunk] = jnp.dot(
            xb, w_ref[:, n0:n0 + n_chunk], preferred_element_type=F32).astype(o_ref.dtype)


def _rowmm(x, w, layer, tm_target=512):
    m, k = x.shape
    n = w.shape[2]
    tm = _tile(m, tm_target)
    n_chunk = 512 if n % 512 == 0 else n
    return pl.pallas_call(
        functools.partial(_rowmm_kernel, n_chunk=n_chunk),
        grid=(m // tm,),
        in_specs=[pl.BlockSpec((tm, k), lambda i: (i, 0)),
                  pl.BlockSpec((None, k, n), lambda i: (layer, 0, 0), pipeline_mode=pl.Buffered(1))],
        out_specs=pl.BlockSpec((tm, n), lambda i: (i, 0)),
        out_shape=jax.ShapeDtypeStruct((m, n), BF16),
        compiler_params=_cparams(("parallel",)),
    )(x, w)


RELAYOUT_ROWS = 8
CH_PASS = 512
SLAB_PITCH = CH_PASS + 8


def _chunks_per_step(bsz):
    assert RELAYOUT_ROWS % bsz == 0, "batch must divide the 8-row relayout tile"
    return RELAYOUT_ROWS // bsz


def _ssm_inproj_kernel(x_ref, wu_ref, wq_ref, xg_ref, q_ref, tt_ref, *, bsz, cpb):
    d = x_ref.shape[-1]
    xb = x_ref[...].reshape(RELAYOUT_ROWS * SSM_CHUNK, d).astype(BF16)
    q = jnp.dot(xb, wq_ref[...], preferred_element_type=F32)
    q_ref[...] = q.reshape(q_ref.shape).astype(q_ref.dtype)
    for n in range(wu_ref.shape[1] // CH_PASS):
        u = jnp.dot(xb, wu_ref[:, n * CH_PASS:(n + 1) * CH_PASS], preferred_element_type=F32)
        for b in range(bsz):
            for cc in range(cpb):
                src = (b * cpb + cc) * SSM_CHUNK
                slot = cc * bsz + b
                tt_ref[slot * SLAB_PITCH:slot * SLAB_PITCH + CH_PASS, :] = u[src:src + SSM_CHUNK, :].T
        for r in range(CH_PASS):
            g = (n * CH_PASS + r) // SSM_GROUP
            c = r % SSM_GROUP
            xg_ref[g, :, c * SSM_CHUNK:(c + 1) * SSM_CHUNK] = tt_ref[pl.ds(r, RELAYOUT_ROWS, stride=SLAB_PITCH), :]


def _ssm_inproj(x, w_in, layer, mix_w):
    bsz, seq, d = x.shape
    nc = seq // SSM_CHUNK
    cpb = _chunks_per_step(bsz)
    assert nc % cpb == 0 and mix_w % CH_PASS == 0
    n_g = mix_w // SSM_GROUP
    mem_w = w_in.shape[2] - mix_w
    x4 = x.reshape(bsz, nc, SSM_CHUNK, d)
    return pl.pallas_call(
        functools.partial(_ssm_inproj_kernel, bsz=bsz, cpb=cpb),
        grid=(nc // cpb,),
        in_specs=[pl.BlockSpec((bsz, cpb, SSM_CHUNK, d), lambda i: (0, i, 0, 0)),
                  pl.BlockSpec((None, d, mix_w), lambda i: (layer, 0, 0), pipeline_mode=pl.Buffered(1)),
                  pl.BlockSpec((None, d, mem_w), lambda i: (layer, 0, mix_w // mem_w),
                               pipeline_mode=pl.Buffered(1))],
        out_specs=[pl.BlockSpec((n_g, RELAYOUT_ROWS, GW), lambda i: (0, i, 0)),
                   pl.BlockSpec((bsz, cpb * SSM_CHUNK, mem_w), lambda i: (0, i, 0))],
        out_shape=[jax.ShapeDtypeStruct((n_g, nc * bsz, GW), F32),
                   jax.ShapeDtypeStruct((bsz, seq, mem_w), BF16)],
        scratch_shapes=[pltpu.VMEM((RELAYOUT_ROWS * SLAB_PITCH, SSM_CHUNK), F32)],
        compiler_params=_cparams(("parallel",)),
    )(x4, w_in, w_in)


def _glu_kernel(z_ref, wa_ref, wg_ref, o_ref, tt_ref, zt_ref, *, bsz, cpb):
    n_out = wa_ref.shape[1]
    for n in range(zt_ref.shape[1] // CH_PASS):
        for r in range(CH_PASS):
            g = (n * CH_PASS + r) // SSM_GROUP
            c = r % SSM_GROUP
            tt_ref[pl.ds(r, RELAYOUT_ROWS, stride=SLAB_PITCH), :] = z_ref[g, :, c * SSM_CHUNK:(c + 1) * SSM_CHUNK]
        for b in range(bsz):
            for cc in range(cpb):
                dst = (b * cpb + cc) * SSM_CHUNK
                slot = cc * bsz + b
                zt_ref[dst:dst + SSM_CHUNK, n * CH_PASS:(n + 1) * CH_PASS] = (
                    tt_ref[slot * SLAB_PITCH:slot * SLAB_PITCH + CH_PASS, :].T.astype(zt_ref.dtype))
    z = zt_ref[...]
    for n0 in range(0, n_out, CH_PASS):
        a = jnp.dot(z, wa_ref[:, n0:n0 + CH_PASS], preferred_element_type=F32)
        g = jnp.dot(z, wg_ref[:, n0:n0 + CH_PASS], preferred_element_type=F32)
        y = (a * jax.nn.sigmoid(g)).astype(o_ref.dtype)
        o_ref[:, :, n0:n0 + CH_PASS] = y.reshape(bsz, cpb * SSM_CHUNK, CH_PASS)


def _glu(z, w_glu, layer, bsz):
    n_g, rows, _ = z.shape
    mix_w = n_g * SSM_GROUP
    cpb = _chunks_per_step(bsz)
    nc = rows // bsz
    return pl.pallas_call(
        functools.partial(_glu_kernel, bsz=bsz, cpb=cpb),
        grid=(nc // cpb,),
        in_specs=[pl.BlockSpec((n_g, RELAYOUT_ROWS, GW), lambda i: (0, i, 0)),
                  pl.BlockSpec((None, mix_w, mix_w), lambda i: (layer, 0, 0), pipeline_mode=pl.Buffered(1)),
                  pl.BlockSpec((None, mix_w, mix_w), lambda i: (layer, 0, 1), pipeline_mode=pl.Buffered(1))],
        out_specs=pl.BlockSpec((bsz, cpb * SSM_CHUNK, mix_w), lambda i: (0, i, 0)),
        out_shape=jax.ShapeDtypeStruct((bsz, nc * SSM_CHUNK, mix_w), BF16),
        scratch_shapes=[pltpu.VMEM((RELAYOUT_ROWS * SLAB_PITCH, SSM_CHUNK), F32),
                        pltpu.VMEM((RELAYOUT_ROWS * SSM_CHUNK, mix_w), BF16)],
        compiler_params=_cparams(("parallel",)),
    )(z, w_glu, w_glu)


def _oproj_kernel(ymix_ref, ymem_ref, wa_ref, wb_ref, x_ref, g_ref, b_ref, o_ref, *, alpha, n_sub):
    sub = x_ref.shape[0] // n_sub
    for r in range(n_sub):
        rs = slice(r * sub, (r + 1) * sub)
        acc = jnp.dot(ymix_ref[rs, :], wa_ref[...], preferred_element_type=F32)
        acc = acc + jnp.dot(ymem_ref[rs, :], wb_ref[...], preferred_element_type=F32)
        y = alpha * x_ref[rs, :] + acc
        o_ref[rs, :] = _layer_norm(y, g_ref[...], b_ref[...])


def _oproj(ymix, ymem, w_out, x, g, b, layer, alpha, tm_target=1024, n_sub=4):
    m, d = x.shape
    kmix = ymix.shape[1]
    kmem = ymem.shape[1]
    tm = _tile(m, tm_target)
    if tm % (8 * n_sub):
        n_sub = 1
    return pl.pallas_call(
        functools.partial(_oproj_kernel, alpha=alpha, n_sub=n_sub),
        grid=(m // tm,),
        in_specs=[pl.BlockSpec((tm, kmix), lambda i: (i, 0)),
                  pl.BlockSpec((tm, kmem), lambda i: (i, 0)),
                  pl.BlockSpec((None, kmix, d), lambda i: (layer, 0, 0), pipeline_mode=pl.Buffered(1)),
                  pl.BlockSpec((None, kmem, d), lambda i: (layer, kmix // kmem, 0), pipeline_mode=pl.Buffered(1)),
                  pl.BlockSpec((tm, d), lambda i: (i, 0)),
                  pl.BlockSpec((None, 1, d), lambda i: (layer, 0, 0)),
                  pl.BlockSpec((None, 1, d), lambda i: (layer, 0, 0))],
        out_specs=pl.BlockSpec((tm, d), lambda i: (i, 0)),
        out_shape=jax.ShapeDtypeStruct((m, d), F32),
        compiler_params=_cparams(("parallel",)),
    )(ymix, ymem, w_out, w_out, x, g, b)


def _ffn_kernel(x_ref, w1_ref, w2_ref, g_ref, b_ref, o_ref, xb_ref, *, alpha, n_sub, f_chunk):
    f = pl.program_id(1)
    last = pl.num_programs(1) - 1
    chunks = [slice(c, c + f_chunk) for c in range(0, w1_ref.shape[1], f_chunk)]

    def ffn_part(xb, cs):
        h = jnp.dot(xb, w1_ref[:, cs], preferred_element_type=F32)
        h = jnp.square(jnp.maximum(h, 0.0)).astype(BF16)
        return jnp.dot(h, w2_ref[cs, :], preferred_element_type=F32)

    @pl.when(f == 0)
    def _():
        xb = x_ref[...].astype(BF16)
        xb_ref[...] = xb
        o_ref[...] = alpha * x_ref[...] + ffn_part(xb, chunks[0])
        for cs in chunks[1:]:
            o_ref[...] += ffn_part(xb, cs)

    @pl.when(jnp.logical_and(f > 0, f < last))
    def _():
        for cs in chunks:
            o_ref[...] += ffn_part(xb_ref[...], cs)

    @pl.when(f == last)
    def _():
        for cs in chunks[:-1]:
            o_ref[...] += ffn_part(xb_ref[...], cs)
        sub = o_ref.shape[0] // n_sub
        for r in range(n_sub):
            rows = slice(r * sub, (r + 1) * sub)
            y = o_ref[rows, :] + ffn_part(xb_ref[rows, :], chunks[-1])
            o_ref[rows, :] = _layer_norm(y, g_ref[...], b_ref[...])


def _ffn(x, w1, w2, g, b, layer, alpha, tm_target=512, tf_target=2048, f_chunk=1024, n_sub=2):
    m, d = x.shape
    dff = w1.shape[2]
    tm = _tile(m, tm_target)
    tf = _tile(dff, tf_target)
    assert dff // tf >= 2, "the kernel has distinct first and last d_ff slices"
    return pl.pallas_call(
        functools.partial(_ffn_kernel, alpha=alpha, n_sub=n_sub if tm % (16 * n_sub) == 0 else 1,
                          f_chunk=f_chunk if tf % f_chunk == 0 else tf),
        grid=(m // tm, dff // tf),
        in_specs=[pl.BlockSpec((tm, d), lambda i, f: (i, 0)),
                  pl.BlockSpec((None, d, tf), lambda i, f: (layer, 0, f)),
                  pl.BlockSpec((None, tf, d), lambda i, f: (layer, f, 0)),
                  pl.BlockSpec((None, 1, d), lambda i, f: (layer, 0, 0)),
                  pl.BlockSpec((None, 1, d), lambda i, f: (layer, 0, 0))],
        out_specs=pl.BlockSpec((tm, d), lambda i, f: (i, 0)),
        out_shape=jax.ShapeDtypeStruct((m, d), F32),
        scratch_shapes=[pltpu.VMEM((tm, d), BF16)],
        compiler_params=_cparams(("parallel", "arbitrary"), vmem_mb=V7X_VMEM_LIMIT_FFN_MB),
    )(x, w1, w2, g, b)


def _memattn_kernel(q_ref, k_ref, v_ref, o_ref):
    scale = HEAD_DIM ** -0.5
    for h in range(N_MEM_HEADS):
        sl = slice(h * HEAD_DIM, (h + 1) * HEAD_DIM)
        s = lax.dot_general(q_ref[0, :, sl], k_ref[0, :, sl], (((1,), (1,)), ((), ())),
                            preferred_element_type=F32) * scale
        m = jnp.max(s, axis=-1, keepdims=True)
        p = jnp.exp(s - m)
        l = jnp.sum(p, axis=-1, keepdims=True)
        o = jnp.dot(p.astype(BF16), v_ref[0, :, sl], preferred_element_type=F32)
        o_ref[0, :, sl] = (o / l).astype(o_ref.dtype)


def _memattn(proj, q_col_block, kv, tq_target=1024):
    bsz, seq, _ = proj.shape
    n_mem = kv.shape[1]
    w = N_MEM_HEADS * HEAD_DIM
    tq = _tile(seq, tq_target)
    return pl.pallas_call(
        _memattn_kernel,
        grid=(bsz, seq // tq),
        in_specs=[pl.BlockSpec((1, tq, w), lambda b, i: (b, i, q_col_block)),
                  pl.BlockSpec((1, n_mem, w), lambda b, i: (b, 0, 0)),
                  pl.BlockSpec((1, n_mem, w), lambda b, i: (b, 0, 1))],
        out_specs=pl.BlockSpec((1, tq, w), lambda b, i: (b, i, 0)),
        out_shape=jax.ShapeDtypeStruct((bsz, seq, w), BF16),
        compiler_params=_cparams(("parallel", "parallel")),
    )(proj, kv, kv)


def _rope(x, cos2, sin2):
    return x * cos2 + pltpu.roll(x, HEAD_DIM // 2, axis=1) * sin2


def _wattn_kernel(sink_ref, q_ref, kp_ref, kc_ref, kn_ref, vp_ref, vc_ref, vn_ref,
                  o_ref, kr_ref, vr_ref, *, tq, seq):
    i = pl.program_id(1)
    base = i * tq
    blk = WINDOW
    nqb = tq // blk

    kr_ref[0:blk, :] = kp_ref[0]
    kr_ref[blk:blk + tq, :] = kc_ref[0]
    kr_ref[blk + tq:, :] = kn_ref[0]
    ones = jnp.ones((tq + 2 * blk, HEAD_DIM), BF16)
    for h in range(N_KV_HEADS):
        src = slice(h * HEAD_DIM, (h + 1) * HEAD_DIM)
        dst = slice(2 * h * HEAD_DIM, (2 * h + 1) * HEAD_DIM)
        vr_ref[0:blk, dst] = vp_ref[0, :, src]
        vr_ref[blk:blk + tq, dst] = vc_ref[0, :, src]
        vr_ref[blk + tq:, dst] = vn_ref[0, :, src]
        vr_ref[:, (2 * h + 1) * HEAD_DIM:(2 * h + 2) * HEAD_DIM] = ones

    r = lax.broadcasted_iota(jnp.int32, (blk, 3 * blk), 0)
    c = lax.broadcasted_iota(jnp.int32, (blk, 3 * blk), 1)
    rel = c - blk - r
    band = jnp.abs(rel) <= WINDOW

    for jb in range(nqb):
        kpos = base + (jb - 1) * blk + c
        valid = band & (kpos >= 0) & (kpos < seq)
        bias = jnp.where(valid, 0.0, NEG_INF).astype(F32)
        for hk in range(N_KV_HEADS):
            ksl = slice(hk * HEAD_DIM, (hk + 1) * HEAD_DIM)
            kw = kr_ref[jb * blk:(jb + 3) * blk, ksl]
            vw = vr_ref[jb * blk:(jb + 3) * blk, 2 * hk * HEAD_DIM:(2 * hk + 2) * HEAD_DIM]
            qs = jnp.concatenate(
                [q_ref[0, jb * blk:(jb + 1) * blk, (hk * GQA_GROUP + g) * HEAD_DIM:(hk * GQA_GROUP + g + 1) * HEAD_DIM]
                 for g in range(GQA_GROUP)], axis=0)
            s = lax.dot_general(qs, kw, (((1,), (1,)), ((), ())), preferred_element_type=F32)
            ps = []
            sink_terms = []
            for g in range(GQA_GROUP):
                sk = sink_ref[hk * GQA_GROUP + g]
                sg = s[g * blk:(g + 1) * blk, :] + bias
                m = jnp.maximum(jnp.max(sg, axis=-1, keepdims=True), sk)
                ps.append(jnp.exp(sg - m).astype(BF16))
                sink_terms.append(jnp.exp(sk - m))
            pcat = jnp.concatenate(ps, axis=0)
            o = jnp.dot(pcat, vw, preferred_element_type=F32)
            for g in range(GQA_GROUP):
                hq = hk * GQA_GROUP + g
                den = o[g * blk:(g + 1) * blk, HEAD_DIM:] + sink_terms[g]
                o_ref[0, jb * blk:(jb + 1) * blk, hq * HEAD_DIM:(hq + 1) * HEAD_DIM] = (
                    o[g * blk:(g + 1) * blk, :HEAD_DIM] / den).astype(o_ref.dtype)


def _attn_inproj_kernel(x_ref, w_ref, cos_ref, sin_ref, o_ref, *, n_q_heads, n_rope_heads, n_chunk):
    xb = x_ref[...].astype(BF16)
    cos2 = cos_ref[...]
    sin2 = sin_ref[...]
    scale = HEAD_DIM ** -0.5
    for n0 in range(0, w_ref.shape[1], n_chunk):
        y = jnp.dot(xb, w_ref[:, n0:n0 + n_chunk], preferred_element_type=F32)
        for c0 in range(0, n_chunk, HEAD_DIM):
            head = (n0 + c0) // HEAD_DIM
            blk = y[:, c0:c0 + HEAD_DIM]
            if head < n_rope_heads:
                blk = _rope(blk, cos2, sin2)
            if head < n_q_heads:
                blk = blk * scale
            o_ref[:, n0 + c0:n0 + c0 + HEAD_DIM] = blk.astype(o_ref.dtype)


def _attn_inproj(x, w, layer, cos2, sin2, seq, tm_target=1024):
    m, k = x.shape
    n = w.shape[2]
    tm = _tile(seq, tm_target)
    n_q_heads = N_KV_HEADS * GQA_GROUP
    return pl.pallas_call(
        functools.partial(_attn_inproj_kernel, n_q_heads=n_q_heads, n_rope_heads=n_q_heads + N_KV_HEADS,
                          n_chunk=4 * HEAD_DIM),
        grid=(m // tm,),
        in_specs=[pl.BlockSpec((tm, k), lambda i: (i, 0)),
                  pl.BlockSpec((None, k, n), lambda i: (layer, 0, 0), pipeline_mode=pl.Buffered(1)),
                  pl.BlockSpec((tm, HEAD_DIM), lambda i: (i % (seq // tm), 0)),
                  pl.BlockSpec((tm, HEAD_DIM), lambda i: (i % (seq // tm), 0))],
        out_specs=pl.BlockSpec((tm, n), lambda i: (i, 0)),
        out_shape=jax.ShapeDtypeStruct((m, n), BF16),
        compiler_params=_cparams(("parallel",)),
    )(x, w, cos2, sin2)


def _wattn(proj, sink, tq_target=1024):
    bsz, seq, _ = proj.shape
    blk = WINDOW
    qw = N_KV_HEADS * GQA_GROUP * HEAD_DIM
    kvw = N_KV_HEADS * HEAD_DIM
    tq = _tile(seq, tq_target)
    r = tq // blk
    nb = seq // blk
    kcol = qw // kvw
    vcol = kcol + 1
    in_specs = [
        pl.BlockSpec(memory_space=pltpu.SMEM),
        pl.BlockSpec((1, tq, qw), lambda b, i: (b, i, 0)),
        pl.BlockSpec((1, blk, kvw), lambda b, i: (b, jnp.maximum(i * r - 1, 0), kcol)),
        pl.BlockSpec((1, tq, kvw), lambda b, i: (b, i, kcol)),
        pl.BlockSpec((1, blk, kvw), lambda b, i: (b, jnp.minimum(i * r + r, nb - 1), kcol)),
        pl.BlockSpec((1, blk, kvw), lambda b, i: (b, jnp.maximum(i * r - 1, 0), vcol)),
        pl.BlockSpec((1, tq, kvw), lambda b, i: (b, i, vcol)),
        pl.BlockSpec((1, blk, kvw), lambda b, i: (b, jnp.minimum(i * r + r, nb - 1), vcol)),
    ]
    return pl.pallas_call(
        functools.partial(_wattn_kernel, tq=tq, seq=seq),
        grid=(bsz, seq // tq),
        in_specs=in_specs,
        out_specs=pl.BlockSpec((1, tq, qw), lambda b, i: (b, i, 0)),
        out_shape=jax.ShapeDtypeStruct((bsz, seq, qw), BF16),
        scratch_shapes=[pltpu.VMEM((tq + 2 * blk, kvw), BF16),
                        pltpu.VMEM((tq + 2 * blk, 2 * kvw), BF16)],
        compiler_params=_cparams(("parallel", "parallel")),
    )(sink, proj, proj, proj, proj, proj, proj, proj)


PREP_GROUPS = 4


def _cpow(e, re, im):
    mg = jnp.exp(e * re)
    return mg * jnp.cos(e * im), mg * jnp.sin(e * im)


def _pow_rows(hi, lo):
    (hi_re, hi_im), (lo_re, lo_im) = hi, lo
    rows_re, rows_im = [], []
    for a in range(hi_re.shape[0]):
        ar = hi_re[a:a + 1, :]
        ai = hi_im[a:a + 1, :]
        rows_re.append(ar * lo_re - ai * lo_im)
        rows_im.append(ar * lo_im + ai * lo_re)
    return jnp.concatenate(rows_re, axis=0), jnp.concatenate(rows_im, axis=0)


def _ssm_prep_kernel(rowp_ref, blre_ref, blim_ref, clre_ref, clim_ref, dcol_ref,
                     ktab_ref, win_ref, wout_ref, atab_ref):
    t_len = SSM_CHUNK
    half = SQ // 2
    lane = lax.broadcasted_iota(jnp.int32, (1, SQ), 1)
    is_re_l = (lane % half) < SSM_STATE
    is_f_l = lane < half

    lam_re = rowp_ref[0, 0, 0:1, :]
    lam_im = rowp_ref[0, 0, 1:2, :]
    dt = jnp.exp(rowp_ref[0, 0, 2:3, :])
    re = lam_re * dt
    im = lam_im * dt
    mag = jnp.exp(re)
    n_re = mag * jnp.cos(im) - 1.0
    n_im = mag * jnp.sin(im)
    den = lam_re * lam_re + lam_im * lam_im
    co_re = (n_re * lam_re + n_im * lam_im) / den
    co_im = (n_im * lam_re - n_re * lam_im) / den
    b_re = blre_ref[0, 0]
    b_im = blim_ref[0, 0]
    bb_re = co_re * b_re - co_im * b_im
    bb_im = co_re * b_im + co_im * b_re

    a_i = lax.broadcasted_iota(jnp.int32, (t_len // 8, SQ), 0)
    i_i = lax.broadcasted_iota(jnp.int32, (8, SQ), 0)
    up, dn = 8 * a_i, t_len - 8 - 8 * a_i
    hi_dn_up = _cpow(jnp.where(is_f_l, dn, up).astype(F32), re, im)
    hi_up_dn = _cpow(jnp.where(is_f_l, up, dn).astype(F32), re, im)
    l_re, l_im = _pow_rows(hi_dn_up, _cpow(jnp.where(is_f_l, 7 - i_i, i_i).astype(F32), re, im))
    p1 = jnp.where(is_re_l, bb_re, bb_im)
    p2 = jnp.where(is_re_l, -bb_im, bb_re)
    for c in range(SSM_GROUP):
        win_ref[0, 0, c * t_len:(c + 1) * t_len, :] = (
            l_re * p1[c:c + 1, :] + l_im * p2[c:c + 1, :]).astype(win_ref.dtype)

    mg_t = jnp.exp(t_len * re)
    a_re = mg_t * jnp.cos(t_len * im)
    a_im = mg_t * jnp.sin(t_len * im)
    atab_ref[0, 0, 0:1, :] = a_re
    atab_ref[0, 0, 1:2, :] = jnp.where(is_re_l, -a_im, a_im)
    atab_ref[0, 0, 2:8, :] = jnp.zeros((6, SQ), F32)

    c_re = clre_ref[0, 0]
    c_im = clim_ref[0, 0]
    blocks = []
    for ci in range(SSM_GROUP):
        br = bb_re[ci:ci + 1, :]
        bi = bb_im[ci:ci + 1, :]
        d_re = c_re * br - c_im * bi
        d_im = c_re * bi + c_im * br
        blocks.append(jnp.where(is_re_l, d_re, -d_im))
    dmat = jnp.concatenate(blocks, axis=0)

    t2_re, t2_im = _pow_rows(hi_up_dn, _cpow(jnp.where(is_f_l, i_i, 8 - i_i).astype(F32), re, im))
    k_row = lax.broadcasted_iota(jnp.int32, (t_len, SQ), 0)
    lt = jnp.where(is_re_l, t2_re, t2_im)
    lt = jnp.where(jnp.logical_and(jnp.logical_not(is_f_l), k_row == 0), 0.0, lt)

    def taps_of(lanes):
        return lax.dot_general(dmat[:, lanes], lt[:, lanes], (((1,), (1,)), ((), ())),
                               preferred_element_type=F32, precision=lax.Precision.HIGHEST)

    k_f = taps_of(slice(0, half))
    k_b = taps_of(slice(half, SQ))
    in_b_re = jnp.logical_and(lane >= half, lane < half + SSM_STATE)
    kb0 = jnp.sum(jnp.where(in_b_re, dmat, 0.0), axis=1, keepdims=True)
    tap0 = lax.broadcasted_iota(jnp.int32, (SSM_GROUP * SSM_GROUP, t_len), 1) == 0
    k_f = k_f + jnp.where(tap0, kb0 + dcol_ref[0, 0], 0.0)
    taps = jnp.concatenate([k_f, k_b], axis=1).astype(BF16).astype(F32)
    bits = pltpu.bitcast(taps, jnp.int32)
    ktab_ref[0, 0] = pltpu.roll(bits, 1, axis=1) | lax.shift_right_logical(bits, 16)

    t3_re, t3_im = _pow_rows(hi_up_dn, _cpow(jnp.where(is_f_l, i_i + 1, 8 - i_i).astype(F32), re, im))
    w_cr = jnp.where(is_re_l, t3_re, -t3_im)
    w_ci = jnp.where(is_re_l, -t3_im, -t3_re)
    for c in range(SSM_GROUP):
        wout_ref[0, 0, c * t_len:(c + 1) * t_len, :] = (
            c_re[c:c + 1, :] * w_cr + c_im[c:c + 1, :] * w_ci).astype(wout_ref.dtype)


def _ssm_prep(lam_re, lam_im, log_dt, b_re, b_im, c_re, c_im, d_skip):
    ns, _, n_g, n_p = lam_re.shape
    cg = SSM_GROUP

    def lanes(a):
        return jnp.concatenate([a[:, 0], a[:, 0], a[:, 1], a[:, 1]], axis=-1)

    ldt = jnp.broadcast_to(log_dt[..., None], lam_re.shape)
    rowp = jnp.stack([lanes(lam_re), lanes(lam_im), lanes(ldt)] + [jnp.zeros((ns, n_g, SQ), F32)] * 5, axis=2)
    bl_re = lanes(jnp.swapaxes(b_re, -1, -2))
    bl_im = lanes(jnp.swapaxes(b_im, -1, -2))
    cl_re = lanes(c_re)
    cl_im = lanes(c_im)
    dcol =(jnp.eye(cg, dtype=F32)[None, None] * d_skip.reshape(ns, n_g, 1, cg)).reshape(ns, n_g, cg * cg, 1)

    gb = PREP_GROUPS if n_g % PREP_GROUPS == 0 else 1

    def spec(*shape):
        nd = len(shape)
        return pl.BlockSpec((1, gb) + shape, lambda j, g: (j, g) + (0,) * nd)

    def prep_groups_kernel(*refs):
        for k in range(gb):
            _ssm_prep_kernel(*[r.at[:, k:k + 1] for r in refs])

    return pl.pallas_call(
        prep_groups_kernel,
        grid=(ns, n_g // gb),
        in_specs=[spec(8, SQ), spec(cg, SQ), spec(cg, SQ), spec(cg, SQ), spec(cg, SQ),
                  spec(cg * cg, 1)],
        out_specs=[spec(cg * cg, 2 * SSM_CHUNK), spec(GW, SQ), spec(GW, SQ), spec(8, SQ)],
        out_shape=[jax.ShapeDtypeStruct((ns, n_g, cg * cg, 2 * SSM_CHUNK), jnp.int32),
                   jax.ShapeDtypeStruct((ns, n_g, GW, SQ), BF16),
                   jax.ShapeDtypeStruct((ns, n_g, GW, SQ), BF16),
                   jax.ShapeDtypeStruct((ns, n_g, 8, SQ), F32)],
        compiler_params=_cparams(("parallel", "parallel")),
    )(rowp, bl_re, bl_im, cl_re, cl_im, dcol)


STATE_GROUPS = 4


def _ssm_state_kernel(*refs, n_in):
    win_ref = refs[n_in]
    for x_ref, s_ref in zip(refs[:n_in], refs[n_in + 1:]):
        for k in range(win_ref.shape[0]):
            s_ref[k] = jnp.dot(x_ref[k].astype(BF16), win_ref[k], preferred_element_type=F32)


def _ssm_state(xgs, win, layer):
    n_g = win.shape[1]
    n_in = len(xgs)
    gb = STATE_GROUPS if n_g % STATE_GROUPS == 0 else 1
    return pl.pallas_call(
        functools.partial(_ssm_state_kernel, n_in=n_in),
        grid=(n_g // gb,),
        in_specs=[pl.BlockSpec((gb, xg.shape[1], GW), lambda g: (g, 0, 0)) for xg in xgs]
        + [pl.BlockSpec((None, gb, GW, SQ), lambda g: (layer, g, 0, 0))],
        out_specs=[pl.BlockSpec((gb, xg.shape[1], SQ), lambda g: (g, 0, 0)) for xg in xgs],
        out_shape=[jax.ShapeDtypeStruct((n_g, xg.shape[1], SQ), F32) for xg in xgs],
        compiler_params=_cparams(("parallel",)),
    )(*xgs, win)


SCAN_GROUPS = 8


def _ssm_scan_kernel(s_ref, a_ref, h_ref, *, n_chunks):
    half = SQ // 2
    nb = s_ref.shape[2]
    for k in range(s_ref.shape[0]):
        a1_f = a_ref[k, 0:1, 0:half]
        a2_f = a_ref[k, 1:2, 0:half]
        a1_b = a_ref[k, 0:1, half:]
        a2_b = a_ref[k, 1:2, half:]
        hf = jnp.zeros((nb, half), F32)
        hb = jnp.zeros((nb, half), F32)
        for i in range(n_chunks):
            j = n_chunks - 1 - i
            h_ref[k, i, :, 0:half] = hf.astype(h_ref.dtype)
            h_ref[k, j, :, half:] = hb.astype(h_ref.dtype)
            hf = a1_f * hf + a2_f * pltpu.roll(hf, SSM_STATE, axis=1) + s_ref[k, i, :, 0:half]
            hb = a1_b * hb + a2_b * pltpu.roll(hb, SSM_STATE, axis=1) + s_ref[k, j, :, half:]


def _ssm_scan(s, atab, layer):
    n_g, n_chunks, nb, _ = s.shape
    gb = SCAN_GROUPS if n_g % SCAN_GROUPS == 0 else 1
    return pl.pallas_call(
        functools.partial(_ssm_scan_kernel, n_chunks=n_chunks),
        grid=(n_g // gb,),
        in_specs=[pl.BlockSpec((gb, n_chunks, nb, SQ), lambda g: (g, 0, 0, 0)),
                  pl.BlockSpec((None, gb, 8, SQ), lambda g: (layer, g, 0, 0))],
        out_specs=pl.BlockSpec((gb, n_chunks, nb, SQ), lambda g: (g, 0, 0, 0)),
        out_shape=jax.ShapeDtypeStruct((n_g, n_chunks, nb, SQ), BF16),
        compiler_params=_cparams(("parallel",)),
    )(s, atab)


def _gelu_tanh(y):
    return 0.5 * y * (1.0 + jnp.tanh(math.sqrt(2.0 / math.pi) * (y + 0.044715 * (y * y * y))))


SSM_OUT_KSLICE = 256


def _ssm_out_kernel(*refs, n_in):
    x_refs = refs[:n_in]
    h_refs = refs[n_in:2 * n_in]
    ktab_ref, wout_ref = refs[2 * n_in:2 * n_in + 2]
    z_refs = refs[2 * n_in + 2:3 * n_in + 2]
    m_ref, xb_ref, y_ref = refs[3 * n_in + 2:]
    t_len = SSM_CHUNK
    offs = [0]
    for x_ref in x_refs:
        offs.append(offs[-1] + x_ref.shape[1])

    for x_ref, h_ref, lo, hi in zip(x_refs, h_refs, offs[:-1], offs[1:]):
        xb_ref[lo:hi, :] = x_ref[0].astype(BF16)
        y_ref[lo:hi, :] = lax.dot_general(h_ref[0], wout_ref[0], (((1,), (1,)), ((), ())),
                                          preferred_element_type=F32)

    for kk in range(GW // SSM_OUT_KSLICE):
        for ci in range(kk * SSM_OUT_KSLICE // t_len, (kk + 1) * SSM_OUT_KSLICE // t_len):
            for co in range(SSM_GROUP):
                r = ci * SSM_GROUP + co
                p8 = jnp.broadcast_to(ktab_ref[0, r:r + 1, :], (8, 2 * t_len))
                for a in range(t_len // 16):
                    w = pltpu.roll(p8, 16 * a, axis=1, stride=2, stride_axis=0)[:, :t_len]
                    m_ref[ci * t_len + 16 * a:ci * t_len + 16 * a + 16, co * t_len:(co + 1) * t_len] = (
                        pltpu.bitcast(w, m_ref.dtype))
        ks = slice(kk * SSM_OUT_KSLICE, (kk + 1) * SSM_OUT_KSLICE)
        y_ref[...] += jnp.dot(xb_ref[:, ks], m_ref[ks, :], preferred_element_type=F32)

    for z_ref, lo, hi in zip(z_refs, offs[:-1], offs[1:]):
        z_ref[0] = _gelu_tanh(y_ref[lo:hi, :]).astype(z_ref.dtype)


def _ssm_out(xgs, hprevs, ktab, wout, layer):
    n_g = ktab.shape[1]
    n_in = len(xgs)
    rows = sum(xg.shape[1] for xg in xgs)
    return pl.pallas_call(
        functools.partial(_ssm_out_kernel, n_in=n_in),
        grid=(n_g,),
        in_specs=[pl.BlockSpec((1, xg.shape[1], GW), lambda g: (g, 0, 0)) for xg in xgs]
        + [pl.BlockSpec((1, h.shape[1], SQ), lambda g: (g, 0, 0)) for h in hprevs]
        + [pl.BlockSpec((None, 1, SSM_GROUP * SSM_GROUP, 2 * SSM_CHUNK), lambda g: (layer, g, 0, 0)),
           pl.BlockSpec((None, 1, GW, SQ), lambda g: (layer, g, 0, 0))],
        out_specs=[pl.BlockSpec((1, xg.shape[1], GW), lambda g: (g, 0, 0)) for xg in xgs],
        out_shape=[jax.ShapeDtypeStruct((n_g, xg.shape[1], GW), F32) for xg in xgs],
        scratch_shapes=[pltpu.VMEM((GW, GW), BF16),
                        pltpu.VMEM((rows, GW), BF16),
                        pltpu.VMEM((rows, GW), F32)],
        compiler_params=_cparams(("parallel",)),
    )(*xgs, *hprevs, ktab, wout)


def _rope_tables(seq):
    inv_freq = ROPE_THETA ** (-jnp.arange(0, HEAD_DIM, 2, dtype=F32) / HEAD_DIM)
    ang = jnp.arange(seq, dtype=F32)[:, None] * inv_freq[None, :]
    cos, sin = jnp.cos(ang), jnp.sin(ang)
    return jnp.concatenate([cos, cos], axis=-1), jnp.concatenate([-sin, sin], axis=-1)


def kernel(x_prompt, x_sample, mem_prompt, mem_sample, ssm_w_in, ssm_lam_re, ssm_lam_im, ssm_log_dt, ssm_b_re,
           ssm_b_im, ssm_c_re, ssm_c_im, ssm_d, ssm_w_glu, attn_w_in, attn_sink, w_mem_kv, w_out, ln1_g, ln1_b,
           w_ff1, w_ff2, ln2_g, ln2_b):
    depth = w_out.shape[0]
    d_model = x_prompt.shape[-1]
    alpha = (2 * depth) ** 0.25
    mix_w = ssm_d.shape[-1]
    n_groups = mix_w // SSM_GROUP
    mem_w = N_MEM_HEADS * HEAD_DIM

    ssm_w_in_b = ssm_w_in.astype(BF16)
    ssm_w_glu_b = ssm_w_glu.astype(BF16)
    attn_w_in_b = attn_w_in.astype(BF16)
    w_mem_kv_b = w_mem_kv.astype(BF16)
    w_out_b = w_out.astype(BF16)
    w_ff1_b = w_ff1.astype(BF16)
    w_ff2_b = w_ff2.astype(BF16)

    ktab, win, wout, atab = _ssm_prep(ssm_lam_re, ssm_lam_im, ssm_log_dt, ssm_b_re, ssm_b_im,
                                      ssm_c_re, ssm_c_im, ssm_d.reshape(-1, n_groups, SSM_GROUP))
    shapes = [x_prompt.shape, x_sample.shape]
    ropes = [_rope_tables(s[1]) for s in shapes]
    xs = [x_prompt.reshape(-1, d_model), x_sample.reshape(-1, d_model)]
    mems = [mem_prompt, mem_sample]

    ln1_g3, ln1_b3, ln2_g3, ln2_b3 = [a.reshape(depth, 1, d_model) for a in (ln1_g, ln1_b, ln2_g, ln2_b)]

    for i in range(depth):
        j = i // 2
        kvs = [_rowmm(m.reshape(-1, d_model), w_mem_kv_b, i).reshape(m.shape[0], m.shape[1], -1) for m in mems]
        if i % 2 == 0:
            xgs, qms = zip(*[_ssm_inproj(x.reshape(s), ssm_w_in_b, j, mix_w) for x, s in zip(xs, shapes)])
            ss = _ssm_state(xgs, win, j)
            hprevs = [_ssm_scan(s.reshape(n_groups, -1, shp[0], SQ), atab, j).reshape(n_groups, -1, SQ)
                      for s, shp in zip(ss, shapes)]
            zs = _ssm_out(xgs, hprevs, ktab, wout, j)
            ymix = [_glu(z, ssm_w_glu_b, j, shp[0]) for z, shp in zip(zs, shapes)]
            ymem = [_memattn(q, 0, kv) for q, kv in zip(qms, kvs)]
        else:
            projs = [_attn_inproj(x, attn_w_in_b, j, cos2, sin2, s[1]).reshape(s[0], s[1], -1)
                     for x, s, (cos2, sin2) in zip(xs, shapes, ropes)]
            ymix = [_wattn(p, attn_sink[j]) for p in projs]
            qcol = (mix_w + 2 * N_KV_HEADS * HEAD_DIM) // mem_w
            ymem = [_memattn(p, qcol, kv) for p, kv in zip(projs, kvs)]
        xs = [_oproj(ym.reshape(-1, mix_w), ye.reshape(-1, mem_w), w_out_b, x, ln1_g3, ln1_b3, i, alpha)
              for ym, ye, x in zip(ymix, ymem, xs)]
        xs = [_ffn(x, w_ff1_b, w_ff2_b, ln2_g3, ln2_b3, i, alpha) for x in xs]

    return (xs[0].reshape(shapes[0]), xs[1].reshape(shapes[1]))
```

```python
import functools
import math

import jax
import jax.numpy as jnp
from jax import lax
from jax.experimental import pallas as pl
from jax.experimental.pallas import tpu as pltpu

F32 = jnp.float32
BF16 = jnp.bfloat16

HEAD_DIM = 128
N_KV_HEADS = 4
GQA_GROUP = 3
WINDOW = 128
N_MEM_HEADS = 4
SSM_GROUP = 16
SSM_STATE = 64
SSM_CHUNK = 128
ROPE_THETA = 10000.0
LN_EPS = 1e-5
NEG_INF = -1e30
SQ = 2 * SSM_STATE * 2
GW = SSM_GROUP * SSM_CHUNK
V7X_VMEM_LIMIT_MB = 56
V7X_VMEM_LIMIT_FFN_MB = 60


def _cparams(sem, vmem_mb=V7X_VMEM_LIMIT_MB):
    return pltpu.CompilerParams(dimension_semantics=sem, vmem_limit_bytes=vmem_mb * 1024 * 1024)


def _tile(n, target):
    if n <= target:
        return n
    for t in range(target, 7, -1):
        if n % t == 0 and t % 8 == 0:
            return t
    return n


def _layer_norm(y, g, b):
    mu = jnp.mean(y, axis=-1, keepdims=True)
    d = y - mu
    var = jnp.mean(d * d, axis=-1, keepdims=True)
    return d * lax.rsqrt(var + LN_EPS) * g + b


def _rowmm_kernel(x_ref, w_ref, o_ref, *, n_chunk):
    xb = x_ref[...].astype(BF16)
    n = w_ref.shape[1]
    for n0 in range(0, n, n_chunk):
        o_ref[:, n0:n0 + n_chunk] = jnp.dot(
            xb, w_ref[:, n0:n0 + n_chunk], preferred_element_type=F32).astype(o_ref.dtype)


def _rowmm(x, w, layer, tm_target=512):
    m, k = x.shape
    n = w.shape[2]
    tm = _tile(m, tm_target)
    n_chunk = 512 if n % 512 == 0 else n
    return pl.pallas_call(
        functools.partial(_rowmm_kernel, n_chunk=n_chunk),
        grid=(m // tm,),
        in_specs=[pl.BlockSpec((tm, k), lambda i: (i, 0)),
                  pl.BlockSpec((None, k, n), lambda i: (layer, 0, 0), pipeline_mode=pl.Buffered(1))],
        out_specs=pl.BlockSpec((tm, n), lambda i: (i, 0)),
        out_shape=jax.ShapeDtypeStruct((m, n), BF16),
        compiler_params=_cparams(("parallel",)),
    )(x, w)


RELAYOUT_ROWS = 8
CH_PASS = 512
SLAB_PITCH = CH_PASS + 8


def _chunks_per_step(bsz):
    assert RELAYOUT_ROWS % bsz == 0, "batch must divide the 8-row relayout tile"
    return RELAYOUT_ROWS // bsz


def _ssm_inproj_kernel(x_ref, wu_ref, wq_ref, xg_ref, q_ref, tt_ref, *, bsz, cpb):
    d = x_ref.shape[-1]
    xb = x_ref[...].reshape(RELAYOUT_ROWS * SSM_CHUNK, d).astype(BF16)
    q = jnp.dot(xb, wq_ref[...], preferred_element_type=F32)
    q_ref[...] = q.reshape(q_ref.shape).astype(q_ref.dtype)
    for n in range(wu_ref.shape[1] // CH_PASS):
        u = jnp.dot(xb, wu_ref[:, n * CH_PASS:(n + 1) * CH_PASS], preferred_element_type=F32)
        for b in range(bsz):
            for cc in range(cpb):
                src = (b * cpb + cc) * SSM_CHUNK
                slot = cc * bsz + b
                tt_ref[slot * SLAB_PITCH:slot * SLAB_PITCH + CH_PASS, :] = u[src:src + SSM_CHUNK, :].T
        for r in range(CH_PASS):
            g = (n * CH_PASS + r) // SSM_GROUP
            c = r % SSM_GROUP
            xg_ref[g, :, c * SSM_CHUNK:(c + 1) * SSM_CHUNK] = tt_ref[pl.ds(r, RELAYOUT_ROWS, stride=SLAB_PITCH), :]


def _ssm_inproj(x, w_in, layer, mix_w):
    bsz, seq, d = x.shape
    nc = seq // SSM_CHUNK
    cpb = _chunks_per_step(bsz)
    assert nc % cpb == 0 and mix_w % CH_PASS == 0
    n_g = mix_w // SSM_GROUP
    mem_w = w_in.shape[2] - mix_w
    x4 = x.reshape(bsz, nc, SSM_CHUNK, d)
    return pl.pallas_call(
        functools.partial(_ssm_inproj_kernel, bsz=bsz, cpb=cpb),
        grid=(nc // cpb,),
        in_specs=[pl.BlockSpec((bsz, cpb, SSM_CHUNK, d), lambda i: (0, i, 0, 0)),
                  pl.BlockSpec((None, d, mix_w), lambda i: (layer, 0, 0), pipeline_mode=pl.Buffered(1)),
                  pl.BlockSpec((None, d, mem_w), lambda i: (layer, 0, mix_w // mem_w),
                               pipeline_mode=pl.Buffered(1))],
        out_specs=[pl.BlockSpec((n_g, RELAYOUT_ROWS, GW), lambda i: (0, i, 0)),
                   pl.BlockSpec((bsz, cpb * SSM_CHUNK, mem_w), lambda i: (0, i, 0))],
        out_shape=[jax.ShapeDtypeStruct((n_g, nc * bsz, GW), F32),
                   jax.ShapeDtypeStruct((bsz, seq, mem_w), BF16)],
        scratch_shapes=[pltpu.VMEM((RELAYOUT_ROWS * SLAB_PITCH, SSM_CHUNK), F32)],
        compiler_params=_cparams(("parallel",)),
    )(x4, w_in, w_in)


def _glu_kernel(z_ref, wa_ref, wg_ref, o_ref, tt_ref, zt_ref, *, bsz, cpb):
    n_out = wa_ref.shape[1]
    for n in range(zt_ref.shape[1] // CH_PASS):
        for r in range(CH_PASS):
            g = (n * CH_PASS + r) // SSM_GROUP
            c = r % SSM_GROUP
            tt_ref[pl.ds(r, RELAYOUT_ROWS, stride=SLAB_PITCH), :] = z_ref[g, :, c * SSM_CHUNK:(c + 1) * SSM_CHUNK]
        for b in range(bsz):
            for cc in range(cpb):
                dst = (b * cpb + cc) * SSM_CHUNK
                slot = cc * bsz + b
                zt_ref[dst:dst + SSM_CHUNK, n * CH_PASS:(n + 1) * CH_PASS] = (
                    tt_ref[slot * SLAB_PITCH:slot * SLAB_PITCH + CH_PASS, :].T.astype(zt_ref.dtype))
    z = zt_ref[...]
    for n0 in range(0, n_out, CH_PASS):
        a = jnp.dot(z, wa_ref[:, n0:n0 + CH_PASS], preferred_element_type=F32)
        g = jnp.dot(z, wg_ref[:, n0:n0 + CH_PASS], preferred_element_type=F32)
        y = (a * jax.nn.sigmoid(g)).astype(o_ref.dtype)
        o_ref[:, :, n0:n0 + CH_PASS] = y.reshape(bsz, cpb * SSM_CHUNK, CH_PASS)


def _glu(z, w_glu, layer, bsz):
    n_g, rows, _ = z.shape
    mix_w = n_g * SSM_GROUP
    cpb = _chunks_per_step(bsz)
    nc = rows // bsz
    return pl.pallas_call(
        functools.partial(_glu_kernel, bsz=bsz, cpb=cpb),
        grid=(nc // cpb,),
        in_specs=[pl.BlockSpec((n_g, RELAYOUT_ROWS, GW), lambda i: (0, i, 0)),
                  pl.BlockSpec((None, mix_w, mix_w), lambda i: (layer, 0, 0), pipeline_mode=pl.Buffered(1)),
                  pl.BlockSpec((None, mix_w, mix_w), lambda i: (layer, 0, 1), pipeline_mode=pl.Buffered(1))],
        out_specs=pl.BlockSpec((bsz, cpb * SSM_CHUNK, mix_w), lambda i: (0, i, 0)),
        out_shape=jax.ShapeDtypeStruct((bsz, nc * SSM_CHUNK, mix_w), BF16),
        scratch_shapes=[pltpu.VMEM((RELAYOUT_ROWS * SLAB_PITCH, SSM_CHUNK), F32),
                        pltpu.VMEM((RELAYOUT_ROWS * SSM_CHUNK, mix_w), BF16)],
        compiler_params=_cparams(("parallel",)),
    )(z, w_glu, w_glu)


def _oproj_kernel(ymix_ref, ymem_ref, wa_ref, wb_ref, x_ref, g_ref, b_ref, o_ref, *, alpha, n_sub):
    sub = x_ref.shape[0] // n_sub
    for r in range(n_sub):
        rs = slice(r * sub, (r + 1) * sub)
        acc = jnp.dot(ymix_ref[rs, :], wa_ref[...], preferred_element_type=F32)
        acc = acc + jnp.dot(ymem_ref[rs, :], wb_ref[...], preferred_element_type=F32)
        y = alpha * x_ref[rs, :] + acc
        o_ref[rs, :] = _layer_norm(y, g_ref[...], b_ref[...])


def _oproj(ymix, ymem, w_out, x, g, b, layer, alpha, tm_target=1024, n_sub=4):
    m, d = x.shape
    kmix = ymix.shape[1]
    kmem = ymem.shape[1]
    tm = _tile(m, tm_target)
    if tm % (8 * n_sub):
        n_sub = 1
    return pl.pallas_call(
        functools.partial(_oproj_kernel, alpha=alpha, n_sub=n_sub),
        grid=(m // tm,),
        in_specs=[pl.BlockSpec((tm, kmix), lambda i: (i, 0)),
                  pl.BlockSpec((tm, kmem), lambda i: (i, 0)),
                  pl.BlockSpec((None, kmix, d), lambda i: (layer, 0, 0), pipeline_mode=pl.Buffered(1)),
                  pl.BlockSpec((None, kmem, d), lambda i: (layer, kmix // kmem, 0), pipeline_mode=pl.Buffered(1)),
                  pl.BlockSpec((tm, d), lambda i: (i, 0)),
                  pl.BlockSpec((None, 1, d), lambda i: (layer, 0, 0)),
                  pl.BlockSpec((None, 1, d), lambda i: (layer, 0, 0))],
        out_specs=pl.BlockSpec((tm, d), lambda i: (i, 0)),
        out_shape=jax.ShapeDtypeStruct((m, d), F32),
        compiler_params=_cparams(("parallel",)),
    )(ymix, ymem, w_out, w_out, x, g, b)


def _ffn_kernel(x_ref, w1_ref, w2_ref, g_ref, b_ref, o_ref, xb_ref, *, alpha, n_sub, f_chunk):
    f = pl.program_id(1)
    last = pl.num_programs(1) - 1
    chunks = [slice(c, c + f_chunk) for c in range(0, w1_ref.shape[1], f_chunk)]

    def ffn_part(xb, cs):
        h = jnp.dot(xb, w1_ref[:, cs], preferred_element_type=F32)
        h = jnp.square(jnp.maximum(h, 0.0)).astype(BF16)
        return jnp.dot(h, w2_ref[cs, :], preferred_element_type=F32)

    @pl.when(f == 0)
    def _():
        xb = x_ref[...].astype(BF16)
        xb_ref[...] = xb
        o_ref[...] = alpha * x_ref[...] + ffn_part(xb, chunks[0])
        for cs in chunks[1:]:
            o_ref[...] += ffn_part(xb, cs)

    @pl.when(jnp.logical_and(f > 0, f < last))
    def _():
        for cs in chunks:
            o_ref[...] += ffn_part(xb_ref[...], cs)

    @pl.when(f == last)
    def _():
        for cs in chunks[:-1]:
            o_ref[...] += ffn_part(xb_ref[...], cs)
        sub = o_ref.shape[0] // n_sub
        for r in range(n_sub):
            rows = slice(r * sub, (r + 1) * sub)
            y = o_ref[rows, :] + ffn_part(xb_ref[rows, :], chunks[-1])
            o_ref[rows, :] = _layer_norm(y, g_ref[...], b_ref[...])


def _ffn(x, w1, w2, g, b, layer, alpha, tm_target=512, tf_target=2048, f_chunk=1024, n_sub=2):
    m, d = x.shape
    dff = w1.shape[2]
    tm = _tile(m, tm_target)
    tf = _tile(dff, tf_target)
    assert dff // tf >= 2, "the kernel has distinct first and last d_ff slices"
    return pl.pallas_call(
        functools.partial(_ffn_kernel, alpha=alpha, n_sub=n_sub if tm % (16 * n_sub) == 0 else 1,
                          f_chunk=f_chunk if tf % f_chunk == 0 else tf),
        grid=(m // tm, dff // tf),
        in_specs=[pl.BlockSpec((tm, d), lambda i, f: (i, 0)),
                  pl.BlockSpec((None, d, tf), lambda i, f: (layer, 0, f)),
                  pl.BlockSpec((None, tf, d), lambda i, f: (layer, f, 0)),
                  pl.BlockSpec((None, 1, d), lambda i, f: (layer, 0, 0)),
                  pl.BlockSpec((None, 1, d), lambda i, f: (layer, 0, 0))],
        out_specs=pl.BlockSpec((tm, d), lambda i, f: (i, 0)),
        out_shape=jax.ShapeDtypeStruct((m, d), F32),
        scratch_shapes=[pltpu.VMEM((tm, d), BF16)],
        compiler_params=_cparams(("parallel", "arbitrary"), vmem_mb=V7X_VMEM_LIMIT_FFN_MB),
    )(x, w1, w2, g, b)


def _memattn_kernel(q_ref, k_ref, v_ref, o_ref):
    scale = HEAD_DIM ** -0.5
    for h in range(N_MEM_HEADS):
        sl = slice(h * HEAD_DIM, (h + 1) * HEAD_DIM)
        s = lax.dot_general(q_ref[0, :, sl], k_ref[0, :, sl], (((1,), (1,)), ((), ())),
                            preferred_element_type=F32) * scale
        m = jnp.max(s, axis=-1, keepdims=True)
        p = jnp.exp(s - m)
        l = jnp.sum(p, axis=-1, keepdims=True)
        o = jnp.dot(p.astype(BF16), v_ref[0, :, sl], preferred_element_type=F32)
        o_ref[0, :, sl] = (o / l).astype(o_ref.dtype)


def _memattn(proj, q_col_block, kv, tq_target=1024):
    bsz, seq, _ = proj.shape
    n_mem = kv.shape[1]
    w = N_MEM_HEADS * HEAD_DIM
    tq = _tile(seq, tq_target)
    return pl.pallas_call(
        _memattn_kernel,
        grid=(bsz, seq // tq),
        in_specs=[pl.BlockSpec((1, tq, w), lambda b, i: (b, i, q_col_block)),
                  pl.BlockSpec((1, n_mem, w), lambda b, i: (b, 0, 0)),
                  pl.BlockSpec((1, n_mem, w), lambda b, i: (b, 0, 1))],
        out_specs=pl.BlockSpec((1, tq, w), lambda b, i: (b, i, 0)),
        out_shape=jax.ShapeDtypeStruct((bsz, seq, w), BF16),
        compiler_params=_cparams(("parallel", "parallel")),
    )(proj, kv, kv)


def _rope(x, cos2, sin2):
    return x * cos2 + pltpu.roll(x, HEAD_DIM // 2, axis=1) * sin2


def _wattn_kernel(sink_ref, q_ref, kp_ref, kc_ref, kn_ref, vp_ref, vc_ref, vn_ref,
                  o_ref, kr_ref, vr_ref, *, tq, seq):
    i = pl.program_id(1)
    base = i * tq
    blk = WINDOW
    nqb = tq // blk

    kr_ref[0:blk, :] = kp_ref[0]
    kr_ref[blk:blk + tq, :] = kc_ref[0]
    kr_ref[blk + tq:, :] = kn_ref[0]
    ones = jnp.ones((tq + 2 * blk, HEAD_DIM), BF16)
    for h in range(N_KV_HEADS):
        src = slice(h * HEAD_DIM, (h + 1) * HEAD_DIM)
        dst = slice(2 * h * HEAD_DIM, (2 * h + 1) * HEAD_DIM)
        vr_ref[0:blk, dst] = vp_ref[0, :, src]
        vr_ref[blk:blk + tq, dst] = vc_ref[0, :, src]
        vr_ref[blk + tq:, dst] = vn_ref[0, :, src]
        vr_ref[:, (2 * h + 1) * HEAD_DIM:(2 * h + 2) * HEAD_DIM] = ones

    r = lax.broadcasted_iota(jnp.int32, (blk, 3 * blk), 0)
    c = lax.broadcasted_iota(jnp.int32, (blk, 3 * blk), 1)
    rel = c - blk - r
    band = jnp.abs(rel) <= WINDOW

    for jb in range(nqb):
        kpos = base + (jb - 1) * blk + c
        valid = band & (kpos >= 0) & (kpos < seq)
        bias = jnp.where(valid, 0.0, NEG_INF).astype(F32)
        for hk in range(N_KV_HEADS):
            ksl = slice(hk * HEAD_DIM, (hk + 1) * HEAD_DIM)
            kw = kr_ref[jb * blk:(jb + 3) * blk, ksl]
            vw = vr_ref[jb * blk:(jb + 3) * blk, 2 * hk * HEAD_DIM:(2 * hk + 2) * HEAD_DIM]
            qs = jnp.concatenate(
                [q_ref[0, jb * blk:(jb + 1) * blk, (hk * GQA_GROUP + g) * HEAD_DIM:(hk * GQA_GROUP + g + 1) * HEAD_DIM]
                 for g in range(GQA_GROUP)], axis=0)
            s = lax.dot_general(qs, kw, (((1,), (1,)), ((), ())), preferred_element_type=F32)
            ps = []
            sink_terms = []
            for g in range(GQA_GROUP):
                sk = sink_ref[hk * GQA_GROUP + g]
                sg = s[g * blk:(g + 1) * blk, :] + bias
                m = jnp.maximum(jnp.max(sg, axis=-1, keepdims=True), sk)
                ps.append(jnp.exp(sg - m).astype(BF16))
                sink_terms.append(jnp.exp(sk - m))
            pcat = jnp.concatenate(ps, axis=0)
            o = jnp.dot(pcat, vw, preferred_element_type=F32)
            for g in range(GQA_GROUP):
                hq = hk * GQA_GROUP + g
                den = o[g * blk:(g + 1) * blk, HEAD_DIM:] + sink_terms[g]
                o_ref[0, jb * blk:(jb + 1) * blk, hq * HEAD_DIM:(hq + 1) * HEAD_DIM] = (
                    o[g * blk:(g + 1) * blk, :HEAD_DIM] / den).astype(o_ref.dtype)


def _attn_inproj_kernel(x_ref, w_ref, cos_ref, sin_ref, o_ref, *, n_q_heads, n_rope_heads, n_chunk):
    xb = x_ref[...].astype(BF16)
    cos2 = cos_ref[...]
    sin2 = sin_ref[...]
    scale = HEAD_DIM ** -0.5
    for n0 in range(0, w_ref.shape[1], n_chunk):
        y = jnp.dot(xb, w_ref[:, n0:n0 + n_chunk], preferred_element_type=F32)
        for c0 in range(0, n_chunk, HEAD_DIM):
            head = (n0 + c0) // HEAD_DIM
            blk = y[:, c0:c0 + HEAD_DIM]
            if head < n_rope_heads:
                blk = _rope(blk, cos2, sin2)
            if head < n_q_heads:
                blk = blk * scale
            o_ref[:, n0 + c0:n0 + c0 + HEAD_DIM] = blk.astype(o_ref.dtype)


def _attn_inproj(x, w, layer, cos2, sin2, seq, tm_target=1024):
    m, k = x.shape
    n = w.shape[2]
    tm = _tile(seq, tm_target)
    n_q_heads = N_KV_HEADS * GQA_GROUP
    return pl.pallas_call(
        functools.partial(_attn_inproj_kernel, n_q_heads=n_q_heads, n_rope_heads=n_q_heads + N_KV_HEADS,
                          n_chunk=4 * HEAD_DIM),
        grid=(m // tm,),
        in_specs=[pl.BlockSpec((tm, k), lambda i: (i, 0)),
                  pl.BlockSpec((None, k, n), lambda i: (layer, 0, 0), pipeline_mode=pl.Buffered(1)),
                  pl.BlockSpec((tm, HEAD_DIM), lambda i: (i % (seq // tm), 0)),
                  pl.BlockSpec((tm, HEAD_DIM), lambda i: (i % (seq // tm), 0))],
        out_specs=pl.BlockSpec((tm, n), lambda i: (i, 0)),
        out_shape=jax.ShapeDtypeStruct((m, n), BF16),
        compiler_params=_cparams(("parallel",)),
    )(x, w, cos2, sin2)


def _wattn(proj, sink, tq_target=1024):
    bsz, seq, _ = proj.shape
    blk = WINDOW
    qw = N_KV_HEADS * GQA_GROUP * HEAD_DIM
    kvw = N_KV_HEADS * HEAD_DIM
    tq = _tile(seq, tq_target)
    r = tq // blk
    nb = seq // blk
    kcol = qw // kvw
    vcol = kcol + 1
    in_specs = [
        pl.BlockSpec(memory_space=pltpu.SMEM),
        pl.BlockSpec((1, tq, qw), lambda b, i: (b, i, 0)),
        pl.BlockSpec((1, blk, kvw), lambda b, i: (b, jnp.maximum(i * r - 1, 0), kcol)),
        pl.BlockSpec((1, tq, kvw), lambda b, i: (b, i, kcol)),
        pl.BlockSpec((1, blk, kvw), lambda b, i: (b, jnp.minimum(i * r + r, nb - 1), kcol)),
        pl.BlockSpec((1, blk, kvw), lambda b, i: (b, jnp.maximum(i * r - 1, 0), vcol)),
        pl.BlockSpec((1, tq, kvw), lambda b, i: (b, i, vcol)),
        pl.BlockSpec((1, blk, kvw), lambda b, i: (b, jnp.minimum(i * r + r, nb - 1), vcol)),
    ]
    return pl.pallas_call(
        functools.partial(_wattn_kernel, tq=tq, seq=seq),
        grid=(bsz, seq // tq),
        in_specs=in_specs,
        out_specs=pl.BlockSpec((1, tq, qw), lambda b, i: (b, i, 0)),
        out_shape=jax.ShapeDtypeStruct((bsz, seq, qw), BF16),
        scratch_shapes=[pltpu.VMEM((tq + 2 * blk, kvw), BF16),
                        pltpu.VMEM((tq + 2 * blk, 2 * kvw), BF16)],
        compiler_params=_cparams(("parallel", "parallel")),
    )(sink, proj, proj, proj, proj, proj, proj, proj)


PREP_GROUPS = 4


def _cpow(e, re, im):
    mg = jnp.exp(e * re)
    return mg * jnp.cos(e * im), mg * jnp.sin(e * im)


def _pow_rows(hi, lo):
    (hi_re, hi_im), (lo_re, lo_im) = hi, lo
    rows_re, rows_im = [], []
    for a in range(hi_re.shape[0]):
        ar = hi_re[a:a + 1, :]
        ai = hi_im[a:a + 1, :]
        rows_re.append(ar * lo_re - ai * lo_im)
        rows_im.append(ar * lo_im + ai * lo_re)
    return jnp.concatenate(rows_re, axis=0), jnp.concatenate(rows_im, axis=0)


def _ssm_prep_kernel(rowp_ref, blre_ref, blim_ref, clre_ref, clim_ref, dcol_ref,
                     ktab_ref, win_ref, wout_ref, atab_ref):
    t_len = SSM_CHUNK
    half = SQ // 2
    lane = lax.broadcasted_iota(jnp.int32, (1, SQ), 1)
    is_re_l = (lane % half) < SSM_STATE
    is_f_l = lane < half

    lam_re = rowp_ref[0, 0, 0:1, :]
    lam_im = rowp_ref[0, 0, 1:2, :]
    dt = jnp.exp(rowp_ref[0, 0, 2:3, :])
    re = lam_re * dt
    im = lam_im * dt
    mag = jnp.exp(re)
    n_re = mag * jnp.cos(im) - 1.0
    n_im = mag * jnp.sin(im)
    den = lam_re * lam_re + lam_im * lam_im
    co_re = (n_re * lam_re + n_im * lam_im) / den
    co_im = (n_im * lam_re - n_re * lam_im) / den
    b_re = blre_ref[0, 0]
    b_im = blim_ref[0, 0]
    bb_re = co_re * b_re - co_im * b_im
    bb_im = co_re * b_im + co_im * b_re

    a_i = lax.broadcasted_iota(jnp.int32, (t_len // 8, SQ), 0)
    i_i = lax.broadcasted_iota(jnp.int32, (8, SQ), 0)
    up, dn = 8 * a_i, t_len - 8 - 8 * a_i
    hi_dn_up = _cpow(jnp.where(is_f_l, dn, up).astype(F32), re, im)
    hi_up_dn = _cpow(jnp.where(is_f_l, up, dn).astype(F32), re, im)
    l_re, l_im = _pow_rows(hi_dn_up, _cpow(jnp.where(is_f_l, 7 - i_i, i_i).astype(F32), re, im))
    p1 = jnp.where(is_re_l, bb_re, bb_im)
    p2 = jnp.where(is_re_l, -bb_im, bb_re)
    for c in range(SSM_GROUP):
        win_ref[0, 0, c * t_len:(c + 1) * t_len, :] = (
            l_re * p1[c:c + 1, :] + l_im * p2[c:c + 1, :]).astype(win_ref.dtype)

    mg_t = jnp.exp(t_len * re)
    a_re = mg_t * jnp.cos(t_len * im)
    a_im = mg_t * jnp.sin(t_len * im)
    atab_ref[0, 0, 0:1, :] = a_re
    atab_ref[0, 0, 1:2, :] = jnp.where(is_re_l, -a_im, a_im)
    atab_ref[0, 0, 2:8, :] = jnp.zeros((6, SQ), F32)

    c_re = clre_ref[0, 0]
    c_im = clim_ref[0, 0]
    blocks = []
    for ci in range(SSM_GROUP):
        br = bb_re[ci:ci + 1, :]
        bi = bb_im[ci:ci + 1, :]
        d_re = c_re * br - c_im * bi
        d_im = c_re * bi + c_im * br
        blocks.append(jnp.where(is_re_l, d_re, -d_im))
    dmat = jnp.concatenate(blocks, axis=0)

    t2_re, t2_im = _pow_rows(hi_up_dn, _cpow(jnp.where(is_f_l, i_i, 8 - i_i).astype(F32), re, im))
    k_row = lax.broadcasted_iota(jnp.int32, (t_len, SQ), 0)
    lt = jnp.where(is_re_l, t2_re, t2_im)
    lt = jnp.where(jnp.logical_and(jnp.logical_not(is_f_l), k_row == 0), 0.0, lt)

    def taps_of(lanes):
        return lax.dot_general(dmat[:, lanes], lt[:, lanes], (((1,), (1,)), ((), ())),
                               preferred_element_type=F32, precision=lax.Precision.HIGHEST)

    k_f = taps_of(slice(0, half))
    k_b = taps_of(slice(half, SQ))
    in_b_re = jnp.logical_and(lane >= half, lane < half + SSM_STATE)
    kb0 = jnp.sum(jnp.where(in_b_re, dmat, 0.0), axis=1, keepdims=True)
    tap0 = lax.broadcasted_iota(jnp.int32, (SSM_GROUP * SSM_GROUP, t_len), 1) == 0
    k_f = k_f + jnp.where(tap0, kb0 + dcol_ref[0, 0], 0.0)
    taps = jnp.concatenate([k_f, k_b], axis=1).astype(BF16).astype(F32)
    bits = pltpu.bitcast(taps, jnp.int32)
    ktab_ref[0, 0] = pltpu.roll(bits, 1, axis=1) | lax.shift_right_logical(bits, 16)

    t3_re, t3_im = _pow_rows(hi_up_dn, _cpow(jnp.where(is_f_l, i_i + 1, 8 - i_i).astype(F32), re, im))
    w_cr = jnp.where(is_re_l, t3_re, -t3_im)
    w_ci = jnp.where(is_re_l, -t3_im, -t3_re)
    for c in range(SSM_GROUP):
        wout_ref[0, 0, c * t_len:(c + 1) * t_len, :] = (
            c_re[c:c + 1, :] * w_cr + c_im[c:c + 1, :] * w_ci).astype(wout_ref.dtype)


def _ssm_prep(lam_re, lam_im, log_dt, b_re, b_im, c_re, c_im, d_skip):
    ns, _, n_g, n_p = lam_re.shape
    cg = SSM_GROUP

    def lanes(a):
        return jnp.concatenate([a[:, 0], a[:, 0], a[:, 1], a[:, 1]], axis=-1)

    ldt = jnp.broadcast_to(log_dt[..., None], lam_re.shape)
    rowp = jnp.stack([lanes(lam_re), lanes(lam_im), lanes(ldt)] + [jnp.zeros((ns, n_g, SQ), F32)] * 5, axis=2)
    bl_re = lanes(jnp.swapaxes(b_re, -1, -2))
    bl_im = lanes(jnp.swapaxes(b_im, -1, -2))
    cl_re = lanes(c_re)
    cl_im = lanes(c_im)
    dcol =(jnp.eye(cg, dtype=F32)[None, None] * d_skip.reshape(ns, n_g, 1, cg)).reshape(ns, n_g, cg * cg, 1)

    gb = PREP_GROUPS if n_g % PREP_GROUPS == 0 else 1

    def spec(*shape):
        nd = len(shape)
        return pl.BlockSpec((1, gb) + shape, lambda j, g: (j, g) + (0,) * nd)

    def prep_groups_kernel(*refs):
        for k in range(gb):
            _ssm_prep_kernel(*[r.at[:, k:k + 1] for r in refs])

    return pl.pallas_call(
        prep_groups_kernel,
        grid=(ns, n_g // gb),
        in_specs=[spec(8, SQ), spec(cg, SQ), spec(cg, SQ), spec(cg, SQ), spec(cg, SQ),
                  spec(cg * cg, 1)],
        out_specs=[spec(cg * cg, 2 * SSM_CHUNK), spec(GW, SQ), spec(GW, SQ), spec(8, SQ)],
        out_shape=[jax.ShapeDtypeStruct((ns, n_g, cg * cg, 2 * SSM_CHUNK), jnp.int32),
                   jax.ShapeDtypeStruct((ns, n_g, GW, SQ), BF16),
                   jax.ShapeDtypeStruct((ns, n_g, GW, SQ), BF16),
                   jax.ShapeDtypeStruct((ns, n_g, 8, SQ), F32)],
        compiler_params=_cparams(("parallel", "parallel")),
    )(rowp, bl_re, bl_im, cl_re, cl_im, dcol)


STATE_GROUPS = 4


def _ssm_state_kernel(*refs, n_in):
    win_ref = refs[n_in]
    for x_ref, s_ref in zip(refs[:n_in], refs[n_in + 1:]):
        for k in range(win_ref.shape[0]):
            s_ref[k] = jnp.dot(x_ref[k].astype(BF16), win_ref[k], preferred_element_type=F32)


def _ssm_state(xgs, win, layer):
    n_g = win.shape[1]
    n_in = len(xgs)
    gb = STATE_GROUPS if n_g % STATE_GROUPS == 0 else 1
    return pl.pallas_call(
        functools.partial(_ssm_state_kernel, n_in=n_in),
        grid=(n_g // gb,),
        in_specs=[pl.BlockSpec((gb, xg.shape[1], GW), lambda g: (g, 0, 0)) for xg in xgs]
        + [pl.BlockSpec((None, gb, GW, SQ), lambda g: (layer, g, 0, 0))],
        out_specs=[pl.BlockSpec((gb, xg.shape[1], SQ), lambda g: (g, 0, 0)) for xg in xgs],
        out_shape=[jax.ShapeDtypeStruct((n_g, xg.shape[1], SQ), F32) for xg in xgs],
        compiler_params=_cparams(("parallel",)),
    )(*xgs, win)


SCAN_GROUPS = 8


def _ssm_scan_kernel(s_ref, a_ref, h_ref, *, n_chunks):
    half = SQ // 2
    nb = s_ref.shape[2]
    for k in range(s_ref.shape[0]):
        a1_f = a_ref[k, 0:1, 0:half]
        a2_f = a_ref[k, 1:2, 0:half]
        a1_b = a_ref[k, 0:1, half:]
        a2_b = a_ref[k, 1:2, half:]
        hf = jnp.zeros((nb, half), F32)
        hb = jnp.zeros((nb, half), F32)
        for i in range(n_chunks):
            j = n_chunks - 1 - i
            h_ref[k, i, :, 0:half] = hf.astype(h_ref.dtype)
            h_ref[k, j, :, half:] = hb.astype(h_ref.dtype)
            hf = a1_f * hf + a2_f * pltpu.roll(hf, SSM_STATE, axis=1) + s_ref[k, i, :, 0:half]
            hb = a1_b * hb + a2_b * pltpu.roll(hb, SSM_STATE, axis=1) + s_ref[k, j, :, half:]


def _ssm_scan(s, atab, layer):
    n_g, n_chunks, nb, _ = s.shape
    gb = SCAN_GROUPS if n_g % SCAN_GROUPS == 0 else 1
    return pl.pallas_call(
        functools.partial(_ssm_scan_kernel, n_chunks=n_chunks),
        grid=(n_g // gb,),
        in_specs=[pl.BlockSpec((gb, n_chunks, nb, SQ), lambda g: (g, 0, 0, 0)),
                  pl.BlockSpec((None, gb, 8, SQ), lambda g: (layer, g, 0, 0))],
        out_specs=pl.BlockSpec((gb, n_chunks, nb, SQ), lambda g: (g, 0, 0, 0)),
        out_shape=jax.ShapeDtypeStruct((n_g, n_chunks, nb, SQ), BF16),
        compiler_params=_cparams(("parallel",)),
    )(s, atab)


def _gelu_tanh(y):
    return 0.5 * y * (1.0 + jnp.tanh(math.sqrt(2.0 / math.pi) * (y + 0.044715 * (y * y * y))))


SSM_OUT_KSLICE = 256


def _chunk_scan(s_ref, h_ref, a_ref, lo, hi, bsz):
    half = SQ // 2
    cpv = 8 // bsz
    n_tiles = (hi - lo) // 8
    blk = lax.broadcasted_iota(jnp.int32, (8, half), 0) // bsz
    for lanes, order, shift in ((slice(0, half), range(n_tiles), bsz % 8),
                                (slice(half, SQ), range(n_tiles - 1, -1, -1), (8 - bsz) % 8)):
        fwd = lanes.start == 0
        a1 = a_ref[0, 0:1, lanes]
        a2 = a_ref[0, 1:2, lanes]
        cur = jnp.zeros((8, half), F32)
        for k in order:
            rows = slice(lo + 8 * k, lo + 8 * k + 8)
            s_k = s_ref[rows, lanes]
            h_k = jnp.zeros((8, half), F32)
            for j in (range(cpv) if fwd else range(cpv - 1, -1, -1)):
                h_k = jnp.where(blk == j, cur, h_k)
                cur = a1 * cur + a2 * pltpu.roll(cur, SSM_STATE, axis=1) + s_k
                if shift:
                    cur = pltpu.roll(cur, shift, axis=0)
            h_ref[rows, lanes] = h_k


def _ssm_core_kernel(*refs, n_in, bszs):
    x_refs = refs[:n_in]
    ktab_ref, win_ref, wout_ref, a_ref = refs[n_in:n_in + 4]
    z_refs = refs[n_in + 4:2 * n_in + 4]
    m_ref, xb_ref, y_ref, s_ref, h_ref = refs[2 * n_in + 4:]
    t_len = SSM_CHUNK
    offs = [0]
    for x_ref in x_refs:
        offs.append(offs[-1] + x_ref.shape[1])

    for x_ref, lo, hi in zip(x_refs, offs[:-1], offs[1:]):
        xb_ref[lo:hi, :] = x_ref[0].astype(BF16)
    s_ref[...] = jnp.dot(xb_ref[...], win_ref[0], preferred_element_type=F32)
    for lo, hi, bsz in zip(offs[:-1], offs[1:], bszs):
        _chunk_scan(s_ref, h_ref, a_ref, lo, hi, bsz)

    for kk in range(GW // SSM_OUT_KSLICE):
        for ci in range(kk * SSM_OUT_KSLICE // t_len, (kk + 1) * SSM_OUT_KSLICE // t_len):
            for co in range(SSM_GROUP):
                r = ci * SSM_GROUP + co
                p8 = jnp.broadcast_to(ktab_ref[0, r:r + 1, :], (8, 2 * t_len))
                for a in range(t_len // 16):
                    w = pltpu.roll(p8, 16 * a, axis=1, stride=2, stride_axis=0)[:, :t_len]
                    m_ref[ci * t_len + 16 * a:ci * t_len + 16 * a + 16, co * t_len:(co + 1) * t_len] = (
                        pltpu.bitcast(w, m_ref.dtype))
        ks = slice(kk * SSM_OUT_KSLICE, (kk + 1) * SSM_OUT_KSLICE)
        part = jnp.dot(xb_ref[:, ks], m_ref[ks, :], preferred_element_type=F32)
        if kk == 0:
            y_ref[...] = part
        else:
            y_ref[...] += part

    carried = lax.dot_general(h_ref[...].astype(BF16), wout_ref[0], (((1,), (1,)), ((), ())),
                              preferred_element_type=F32)
    for z_ref, lo, hi in zip(z_refs, offs[:-1], offs[1:]):
        z_ref[0] = _gelu_tanh(y_ref[lo:hi, :] + carried[lo:hi, :]).astype(z_ref.dtype)


def _ssm_core(xgs, bszs, ktab, win, wout, atab, layer):
    n_g = ktab.shape[1]
    n_in = len(xgs)
    rows = sum(xg.shape[1] for xg in xgs)
    assert all(8 % b == 0 and xg.shape[1] % 8 == 0 for b, xg in zip(bszs, xgs))

    def table(*shape):
        return pl.BlockSpec((None, 1) + shape, lambda g: (layer, g, 0, 0))

    return pl.pallas_call(
        functools.partial(_ssm_core_kernel, n_in=n_in, bszs=tuple(bszs)),
        grid=(n_g,),
        in_specs=[pl.BlockSpec((1, xg.shape[1], GW), lambda g: (g, 0, 0)) for xg in xgs]
        + [table(SSM_GROUP * SSM_GROUP, 2 * SSM_CHUNK), table(GW, SQ), table(GW, SQ), table(8, SQ)],
        out_specs=[pl.BlockSpec((1, xg.shape[1], GW), lambda g: (g, 0, 0)) for xg in xgs],
        out_shape=[jax.ShapeDtypeStruct((n_g, xg.shape[1], GW), F32) for xg in xgs],
        scratch_shapes=[pltpu.VMEM((GW, GW), BF16),
                        pltpu.VMEM((rows, GW), BF16),
                        pltpu.VMEM((rows, GW), F32),
                        pltpu.VMEM((rows, SQ), F32),
                        pltpu.VMEM((rows, SQ), F32)],
        compiler_params=_cparams(("parallel",)),
    )(*xgs, ktab, win, wout, atab)


def _rope_tables(seq):
    inv_freq = ROPE_THETA ** (-jnp.arange(0, HEAD_DIM, 2, dtype=F32) / HEAD_DIM)
    ang = jnp.arange(seq, dtype=F32)[:, None] * inv_freq[None, :]
    cos, sin = jnp.cos(ang), jnp.sin(ang)
    return jnp.concatenate([cos, cos], axis=-1), jnp.concatenate([-sin, sin], axis=-1)


def kernel(x_prompt, x_sample, mem_prompt, mem_sample, ssm_w_in, ssm_lam_re, ssm_lam_im, ssm_log_dt, ssm_b_re,
           ssm_b_im, ssm_c_re, ssm_c_im, ssm_d, ssm_w_glu, attn_w_in, attn_sink, w_mem_kv, w_out, ln1_g, ln1_b,
           w_ff1, w_ff2, ln2_g, ln2_b):
    depth = w_out.shape[0]
    d_model = x_prompt.shape[-1]
    alpha = (2 * depth) ** 0.25
    mix_w = ssm_d.shape[-1]
    n_groups = mix_w // SSM_GROUP
    mem_w = N_MEM_HEADS * HEAD_DIM

    ssm_w_in_b = ssm_w_in.astype(BF16)
    ssm_w_glu_b = ssm_w_glu.astype(BF16)
    attn_w_in_b = attn_w_in.astype(BF16)
    w_mem_kv_b = w_mem_kv.astype(BF16)
    w_out_b = w_out.astype(BF16)
    w_ff1_b = w_ff1.astype(BF16)
    w_ff2_b = w_ff2.astype(BF16)

    ktab, win, wout, atab = _ssm_prep(ssm_lam_re, ssm_lam_im, ssm_log_dt, ssm_b_re, ssm_b_im,
                                      ssm_c_re, ssm_c_im, ssm_d.reshape(-1, n_groups, SSM_GROUP))
    shapes = [x_prompt.shape, x_sample.shape]
    ropes = [_rope_tables(s[1]) for s in shapes]
    xs = [x_prompt.reshape(-1, d_model), x_sample.reshape(-1, d_model)]
    mems = [mem_prompt, mem_sample]

    ln1_g3, ln1_b3, ln2_g3, ln2_b3 = [a.reshape(depth, 1, d_model) for a in (ln1_g, ln1_b, ln2_g, ln2_b)]

    for i in range(depth):
        j = i // 2
        kvs = [_rowmm(m.reshape(-1, d_model), w_mem_kv_b, i).reshape(m.shape[0], m.shape[1], -1) for m in mems]
        if i % 2 == 0:
            xgs, qms = zip(*[_ssm_inproj(x.reshape(s), ssm_w_in_b, j, mix_w) for x, s in zip(xs, shapes)])
            zs = _ssm_core(xgs, [shp[0] for shp in shapes], ktab, win, wout, atab, j)
            ymix = [_glu(z, ssm_w_glu_b, j, shp[0]) for z, shp in zip(zs, shapes)]
            ymem = [_memattn(q, 0, kv) for q, kv in zip(qms, kvs)]
        else:
            projs = [_attn_inproj(x, attn_w_in_b, j, cos2, sin2, s[1]).reshape(s[0], s[1], -1)
                     for x, s, (cos2, sin2) in zip(xs, shapes, ropes)]
            ymix = [_wattn(p, attn_sink[j]) for p in projs]
            qcol = (mix_w + 2 * N_KV_HEADS * HEAD_DIM) // mem_w
            ymem = [_memattn(p, qcol, kv) for p, kv in zip(projs, kvs)]
        xs = [_oproj(ym.reshape(-1, mix_w), ye.reshape(-1, mem_w), w_out_b, x, ln1_g3, ln1_b3, i, alpha)
              for ym, ye, x in zip(ymix, ymem, xs)]
        xs = [_ffn(x, w_ff1_b, w_ff2_b, ln2_g3, ln2_b3, i, alpha) for x in xs]

    return (xs[0].reshape(shapes[0]), xs[1].reshape(shapes[1]))
```

```python
import functools
import math

import jax
import jax.numpy as jnp
from jax import lax
from jax.experimental import pallas as pl
from jax.experimental.pallas import tpu as pltpu

F32 = jnp.float32
BF16 = jnp.bfloat16

HEAD_DIM = 128
N_KV_HEADS = 4
GQA_GROUP = 3
WINDOW = 128
N_MEM_HEADS = 4
SSM_GROUP = 16
SSM_STATE = 64
SSM_CHUNK = 128
ROPE_THETA = 10000.0
LN_EPS = 1e-5
NEG_INF = -1e30
SQ = 2 * SSM_STATE * 2
GW = SSM_GROUP * SSM_CHUNK
V7X_VMEM_LIMIT_MB = 56
V7X_VMEM_LIMIT_FFN_MB = 60


def _cparams(sem, vmem_mb=V7X_VMEM_LIMIT_MB):
    return pltpu.CompilerParams(dimension_semantics=sem, vmem_limit_bytes=vmem_mb * 1024 * 1024)


def _tile(n, target):
    if n <= target:
        return n
    for t in range(target, 7, -1):
        if n % t == 0 and t % 8 == 0:
            return t
    return n


def _layer_norm(y, g, b):
    mu = jnp.mean(y, axis=-1, keepdims=True)
    d = y - mu
    var = jnp.mean(d * d, axis=-1, keepdims=True)
    return d * lax.rsqrt(var + LN_EPS) * g + b


def _rowmm_kernel(x_ref, w_ref, o_ref, *, n_chunk):
    xb = x_ref[...].astype(BF16)
    n = w_ref.shape[1]
    for n0 in range(0, n, n_chunk):
        o_ref[:, n0:n0 + n_chunk] = jnp.dot(
            xb, w_ref[:, n0:n0 + n_chunk], preferred_element_type=F32).astype(o_ref.dtype)


def _rowmm(x, w, layer, tm_target=512):
    m, k = x.shape
    n = w.shape[2]
    tm = _tile(m, tm_target)
    n_chunk = 512 if n % 512 == 0 else n
    return pl.pallas_call(
        functools.partial(_rowmm_kernel, n_chunk=n_chunk),
        grid=(m // tm,),
        in_specs=[pl.BlockSpec((tm, k), lambda i: (i, 0)),
                  pl.BlockSpec((None, k, n), lambda i: (layer, 0, 0), pipeline_mode=pl.Buffered(1))],
        out_specs=pl.BlockSpec((tm, n), lambda i: (i, 0)),
        out_shape=jax.ShapeDtypeStruct((m, n), BF16),
        compiler_params=_cparams(("parallel",)),
    )(x, w)


RELAYOUT_ROWS = 8
CH_PASS = 512
SLAB_PITCH = CH_PASS + 8


def _chunks_per_step(bsz):
    assert RELAYOUT_ROWS % bsz == 0, "batch must divide the 8-row relayout tile"
    return RELAYOUT_ROWS // bsz


def _ssm_inproj_kernel(x_ref, wu_ref, wq_ref, xg_ref, q_ref, tt_ref, *, bsz, cpb):
    d = x_ref.shape[-1]
    xb = x_ref[...].reshape(RELAYOUT_ROWS * SSM_CHUNK, d).astype(BF16)
    q = jnp.dot(xb, wq_ref[...], preferred_element_type=F32)
    q_ref[...] = q.reshape(q_ref.shape).astype(q_ref.dtype)
    for n in range(wu_ref.shape[1] // CH_PASS):
        u = jnp.dot(xb, wu_ref[:, n * CH_PASS:(n + 1) * CH_PASS], preferred_element_type=F32)
        for b in range(bsz):
            for cc in range(cpb):
                src = (b * cpb + cc) * SSM_CHUNK
                slot = cc * bsz + b
                tt_ref[slot * SLAB_PITCH:slot * SLAB_PITCH + CH_PASS, :] = u[src:src + SSM_CHUNK, :].T
        for r in range(CH_PASS):
            g = (n * CH_PASS + r) // SSM_GROUP
            c = r % SSM_GROUP
            xg_ref[g, :, c * SSM_CHUNK:(c + 1) * SSM_CHUNK] = tt_ref[pl.ds(r, RELAYOUT_ROWS, stride=SLAB_PITCH), :]


def _ssm_inproj(x, w_in, layer, mix_w):
    bsz, seq, d = x.shape
    nc = seq // SSM_CHUNK
    cpb = _chunks_per_step(bsz)
    assert nc % cpb == 0 and mix_w % CH_PASS == 0
    n_g = mix_w // SSM_GROUP
    mem_w = w_in.shape[2] - mix_w
    x4 = x.reshape(bsz, nc, SSM_CHUNK, d)
    return pl.pallas_call(
        functools.partial(_ssm_inproj_kernel, bsz=bsz, cpb=cpb),
        grid=(nc // cpb,),
        in_specs=[pl.BlockSpec((bsz, cpb, SSM_CHUNK, d), lambda i: (0, i, 0, 0)),
                  pl.BlockSpec((None, d, mix_w), lambda i: (layer, 0, 0), pipeline_mode=pl.Buffered(1)),
                  pl.BlockSpec((None, d, mem_w), lambda i: (layer, 0, mix_w // mem_w),
                               pipeline_mode=pl.Buffered(1))],
        out_specs=[pl.BlockSpec((n_g, RELAYOUT_ROWS, GW), lambda i: (0, i, 0)),
                   pl.BlockSpec((bsz, cpb * SSM_CHUNK, mem_w), lambda i: (0, i, 0))],
        out_shape=[jax.ShapeDtypeStruct((n_g, nc * bsz, GW), F32),
                   jax.ShapeDtypeStruct((bsz, seq, mem_w), BF16)],
        scratch_shapes=[pltpu.VMEM((RELAYOUT_ROWS * SLAB_PITCH, SSM_CHUNK), F32)],
        compiler_params=_cparams(("parallel",)),
    )(x4, w_in, w_in)


def _glu_kernel(z_ref, wa_ref, wg_ref, o_ref, tt_ref, zt_ref, *, bsz, cpb):
    n_out = wa_ref.shape[1]
    for n in range(zt_ref.shape[1] // CH_PASS):
        for r in range(CH_PASS):
            g = (n * CH_PASS + r) // SSM_GROUP
            c = r % SSM_GROUP
            tt_ref[pl.ds(r, RELAYOUT_ROWS, stride=SLAB_PITCH), :] = z_ref[g, :, c * SSM_CHUNK:(c + 1) * SSM_CHUNK]
        for b in range(bsz):
            for cc in range(cpb):
                dst = (b * cpb + cc) * SSM_CHUNK
                slot = cc * bsz + b
                zt_ref[dst:dst + SSM_CHUNK, n * CH_PASS:(n + 1) * CH_PASS] = (
                    tt_ref[slot * SLAB_PITCH:slot * SLAB_PITCH + CH_PASS, :].T.astype(zt_ref.dtype))
    z = zt_ref[...]
    for n0 in range(0, n_out, CH_PASS):
        a = jnp.dot(z, wa_ref[:, n0:n0 + CH_PASS], preferred_element_type=F32)
        g = jnp.dot(z, wg_ref[:, n0:n0 + CH_PASS], preferred_element_type=F32)
        y = (a * jax.nn.sigmoid(g)).astype(o_ref.dtype)
        o_ref[:, :, n0:n0 + CH_PASS] = y.reshape(bsz, cpb * SSM_CHUNK, CH_PASS)


def _glu(z, w_glu, layer, bsz):
    n_g, rows, _ = z.shape
    mix_w = n_g * SSM_GROUP
    cpb = _chunks_per_step(bsz)
    nc = rows // bsz
    return pl.pallas_call(
        functools.partial(_glu_kernel, bsz=bsz, cpb=cpb),
        grid=(nc // cpb,),
        in_specs=[pl.BlockSpec((n_g, RELAYOUT_ROWS, GW), lambda i: (0, i, 0)),
                  pl.BlockSpec((None, mix_w, mix_w), lambda i: (layer, 0, 0), pipeline_mode=pl.Buffered(1)),
                  pl.BlockSpec((None, mix_w, mix_w), lambda i: (layer, 0, 1), pipeline_mode=pl.Buffered(1))],
        out_specs=pl.BlockSpec((bsz, cpb * SSM_CHUNK, mix_w), lambda i: (0, i, 0)),
        out_shape=jax.ShapeDtypeStruct((bsz, nc * SSM_CHUNK, mix_w), BF16),
        scratch_shapes=[pltpu.VMEM((RELAYOUT_ROWS * SLAB_PITCH, SSM_CHUNK), F32),
                        pltpu.VMEM((RELAYOUT_ROWS * SSM_CHUNK, mix_w), BF16)],
        compiler_params=_cparams(("parallel",)),
    )(z, w_glu, w_glu)


def _oproj_kernel(ymix_ref, ymem_ref, wa_ref, wb_ref, x_ref, g_ref, b_ref, o_ref, *, alpha, n_sub):
    sub = x_ref.shape[0] // n_sub
    for r in range(n_sub):
        rs = slice(r * sub, (r + 1) * sub)
        acc = jnp.dot(ymix_ref[rs, :], wa_ref[...], preferred_element_type=F32)
        acc = acc + jnp.dot(ymem_ref[rs, :], wb_ref[...], preferred_element_type=F32)
        y = alpha * x_ref[rs, :] + acc
        o_ref[rs, :] = _layer_norm(y, g_ref[...], b_ref[...])


def _oproj(ymix, ymem, w_out, x, g, b, layer, alpha, tm_target=1024, n_sub=4):
    m, d = x.shape
    kmix = ymix.shape[1]
    kmem = ymem.shape[1]
    tm = _tile(m, tm_target)
    if tm % (8 * n_sub):
        n_sub = 1
    return pl.pallas_call(
        functools.partial(_oproj_kernel, alpha=alpha, n_sub=n_sub),
        grid=(m // tm,),
        in_specs=[pl.BlockSpec((tm, kmix), lambda i: (i, 0)),
                  pl.BlockSpec((tm, kmem), lambda i: (i, 0)),
                  pl.BlockSpec((None, kmix, d), lambda i: (layer, 0, 0), pipeline_mode=pl.Buffered(1)),
                  pl.BlockSpec((None, kmem, d), lambda i: (layer, kmix // kmem, 0), pipeline_mode=pl.Buffered(1)),
                  pl.BlockSpec((tm, d), lambda i: (i, 0)),
                  pl.BlockSpec((None, 1, d), lambda i: (layer, 0, 0)),
                  pl.BlockSpec((None, 1, d), lambda i: (layer, 0, 0))],
        out_specs=pl.BlockSpec((tm, d), lambda i: (i, 0)),
        out_shape=jax.ShapeDtypeStruct((m, d), F32),
        compiler_params=_cparams(("parallel",)),
    )(ymix, ymem, w_out, w_out, x, g, b)


def _ffn_kernel(x_ref, w1_ref, w2_ref, g_ref, b_ref, o_ref, xb_ref, *, alpha, n_sub, f_chunk):
    f = pl.program_id(1)
    last = pl.num_programs(1) - 1
    chunks = [slice(c, c + f_chunk) for c in range(0, w1_ref.shape[1], f_chunk)]

    def ffn_part(xb, cs):
        h = jnp.dot(xb, w1_ref[:, cs], preferred_element_type=F32)
        h = jnp.square(jnp.maximum(h, 0.0)).astype(BF16)
        return jnp.dot(h, w2_ref[cs, :], preferred_element_type=F32)

    @pl.when(f == 0)
    def _():
        xb = x_ref[...].astype(BF16)
        xb_ref[...] = xb
        o_ref[...] = alpha * x_ref[...] + ffn_part(xb, chunks[0])
        for cs in chunks[1:]:
            o_ref[...] += ffn_part(xb, cs)

    @pl.when(jnp.logical_and(f > 0, f < last))
    def _():
        for cs in chunks:
            o_ref[...] += ffn_part(xb_ref[...], cs)

    @pl.when(f == last)
    def _():
        for cs in chunks[:-1]:
            o_ref[...] += ffn_part(xb_ref[...], cs)
        sub = o_ref.shape[0] // n_sub
        for r in range(n_sub):
            rows = slice(r * sub, (r + 1) * sub)
            y = o_ref[rows, :] + ffn_part(xb_ref[rows, :], chunks[-1])
            o_ref[rows, :] = _layer_norm(y, g_ref[...], b_ref[...])


def _ffn(x, w1, w2, g, b, layer, alpha, tm_target=512, tf_target=2048, f_chunk=1024, n_sub=2):
    m, d = x.shape
    dff = w1.shape[2]
    tm = _tile(m, tm_target)
    tf = _tile(dff, tf_target)
    assert dff // tf >= 2, "the kernel has distinct first and last d_ff slices"
    return pl.pallas_call(
        functools.partial(_ffn_kernel, alpha=alpha, n_sub=n_sub if tm % (16 * n_sub) == 0 else 1,
                          f_chunk=f_chunk if tf % f_chunk == 0 else tf),
        grid=(m // tm, dff // tf),
        in_specs=[pl.BlockSpec((tm, d), lambda i, f: (i, 0)),
                  pl.BlockSpec((None, d, tf), lambda i, f: (layer, 0, f)),
                  pl.BlockSpec((None, tf, d), lambda i, f: (layer, f, 0)),
                  pl.BlockSpec((None, 1, d), lambda i, f: (layer, 0, 0)),
                  pl.BlockSpec((None, 1, d), lambda i, f: (layer, 0, 0))],
        out_specs=pl.BlockSpec((tm, d), lambda i, f: (i, 0)),
        out_shape=jax.ShapeDtypeStruct((m, d), F32),
        scratch_shapes=[pltpu.VMEM((tm, d), BF16)],
        compiler_params=_cparams(("parallel", "arbitrary"), vmem_mb=V7X_VMEM_LIMIT_FFN_MB),
    )(x, w1, w2, g, b)


def _memattn_kernel(q_ref, k_ref, v_ref, o_ref):
    scale = HEAD_DIM ** -0.5
    for h in range(N_MEM_HEADS):
        sl = slice(h * HEAD_DIM, (h + 1) * HEAD_DIM)
        s = lax.dot_general(q_ref[0, :, sl], k_ref[0, :, sl], (((1,), (1,)), ((), ())),
                            preferred_element_type=F32) * scale
        m = jnp.max(s, axis=-1, keepdims=True)
        p = jnp.exp(s - m)
        l = jnp.sum(p, axis=-1, keepdims=True)
        o = jnp.dot(p.astype(BF16), v_ref[0, :, sl], preferred_element_type=F32)
        o_ref[0, :, sl] = (o / l).astype(o_ref.dtype)


def _memattn(proj, q_col_block, kv, tq_target=1024):
    bsz, seq, _ = proj.shape
    n_mem = kv.shape[1]
    w = N_MEM_HEADS * HEAD_DIM
    tq = _tile(seq, tq_target)
    return pl.pallas_call(
        _memattn_kernel,
        grid=(bsz, seq // tq),
        in_specs=[pl.BlockSpec((1, tq, w), lambda b, i: (b, i, q_col_block)),
                  pl.BlockSpec((1, n_mem, w), lambda b, i: (b, 0, 0)),
                  pl.BlockSpec((1, n_mem, w), lambda b, i: (b, 0, 1))],
        out_specs=pl.BlockSpec((1, tq, w), lambda b, i: (b, i, 0)),
        out_shape=jax.ShapeDtypeStruct((bsz, seq, w), BF16),
        compiler_params=_cparams(("parallel", "parallel")),
    )(proj, kv, kv)


def _rope(x, cos2, sin2):
    return x * cos2 + pltpu.roll(x, HEAD_DIM // 2, axis=1) * sin2


def _wattn_kernel(sink_ref, q_ref, kp_ref, kc_ref, kn_ref, vp_ref, vc_ref, vn_ref,
                  o_ref, kr_ref, vr_ref, *, tq, seq):
    i = pl.program_id(1)
    base = i * tq
    blk = WINDOW
    nqb = tq // blk

    kr_ref[0:blk, :] = kp_ref[0]
    kr_ref[blk:blk + tq, :] = kc_ref[0]
    kr_ref[blk + tq:, :] = kn_ref[0]
    ones = jnp.ones((tq + 2 * blk, HEAD_DIM), BF16)
    for h in range(N_KV_HEADS):
        src = slice(h * HEAD_DIM, (h + 1) * HEAD_DIM)
        dst = slice(2 * h * HEAD_DIM, (2 * h + 1) * HEAD_DIM)
        vr_ref[0:blk, dst] = vp_ref[0, :, src]
        vr_ref[blk:blk + tq, dst] = vc_ref[0, :, src]
        vr_ref[blk + tq:, dst] = vn_ref[0, :, src]
        vr_ref[:, (2 * h + 1) * HEAD_DIM:(2 * h + 2) * HEAD_DIM] = ones

    r = lax.broadcasted_iota(jnp.int32, (blk, 3 * blk), 0)
    c = lax.broadcasted_iota(jnp.int32, (blk, 3 * blk), 1)
    rel = c - blk - r
    band = jnp.abs(rel) <= WINDOW

    for jb in range(nqb):
        kpos = base + (jb - 1) * blk + c
        valid = band & (kpos >= 0) & (kpos < seq)
        bias = jnp.where(valid, 0.0, NEG_INF).astype(F32)
        for hk in range(N_KV_HEADS):
            ksl = slice(hk * HEAD_DIM, (hk + 1) * HEAD_DIM)
            kw = kr_ref[jb * blk:(jb + 3) * blk, ksl]
            vw = vr_ref[jb * blk:(jb + 3) * blk, 2 * hk * HEAD_DIM:(2 * hk + 2) * HEAD_DIM]
            qs = jnp.concatenate(
                [q_ref[0, jb * blk:(jb + 1) * blk, (hk * GQA_GROUP + g) * HEAD_DIM:(hk * GQA_GROUP + g + 1) * HEAD_DIM]
                 for g in range(GQA_GROUP)], axis=0)
            s = lax.dot_general(qs, kw, (((1,), (1,)), ((), ())), preferred_element_type=F32)
            ps = []
            sink_terms = []
            for g in range(GQA_GROUP):
                sk = sink_ref[hk * GQA_GROUP + g]
                sg = s[g * blk:(g + 1) * blk, :] + bias
                m = jnp.maximum(jnp.max(sg, axis=-1, keepdims=True), sk)
                ps.append(jnp.exp(sg - m).astype(BF16))
                sink_terms.append(jnp.exp(sk - m))
            pcat = jnp.concatenate(ps, axis=0)
            o = jnp.dot(pcat, vw, preferred_element_type=F32)
            for g in range(GQA_GROUP):
                hq = hk * GQA_GROUP + g
                den = o[g * blk:(g + 1) * blk, HEAD_DIM:] + sink_terms[g]
                o_ref[0, jb * blk:(jb + 1) * blk, hq * HEAD_DIM:(hq + 1) * HEAD_DIM] = (
                    o[g * blk:(g + 1) * blk, :HEAD_DIM] / den).astype(o_ref.dtype)


def _attn_inproj_kernel(x_ref, w_ref, cos_ref, sin_ref, o_ref, *, n_q_heads, n_rope_heads, n_chunk):
    xb = x_ref[...].astype(BF16)
    cos2 = cos_ref[...]
    sin2 = sin_ref[...]
    scale = HEAD_DIM ** -0.5
    for n0 in range(0, w_ref.shape[1], n_chunk):
        y = jnp.dot(xb, w_ref[:, n0:n0 + n_chunk], preferred_element_type=F32)
        for c0 in range(0, n_chunk, HEAD_DIM):
            head = (n0 + c0) // HEAD_DIM
            blk = y[:, c0:c0 + HEAD_DIM]
            if head < n_rope_heads:
                blk = _rope(blk, cos2, sin2)
            if head < n_q_heads:
                blk = blk * scale
            o_ref[:, n0 + c0:n0 + c0 + HEAD_DIM] = blk.astype(o_ref.dtype)


def _attn_inproj(x, w, layer, cos2, sin2, seq, tm_target=1024):
    m, k = x.shape
    n = w.shape[2]
    tm = _tile(seq, tm_target)
    n_q_heads = N_KV_HEADS * GQA_GROUP
    return pl.pallas_call(
        functools.partial(_attn_inproj_kernel, n_q_heads=n_q_heads, n_rope_heads=n_q_heads + N_KV_HEADS,
                          n_chunk=4 * HEAD_DIM),
        grid=(m // tm,),
        in_specs=[pl.BlockSpec((tm, k), lambda i: (i, 0)),
                  pl.BlockSpec((None, k, n), lambda i: (layer, 0, 0), pipeline_mode=pl.Buffered(1)),
                  pl.BlockSpec((tm, HEAD_DIM), lambda i: (i % (seq // tm), 0)),
                  pl.BlockSpec((tm, HEAD_DIM), lambda i: (i % (seq // tm), 0))],
        out_specs=pl.BlockSpec((tm, n), lambda i: (i, 0)),
        out_shape=jax.ShapeDtypeStruct((m, n), BF16),
        compiler_params=_cparams(("parallel",)),
    )(x, w, cos2, sin2)


def _wattn(proj, sink, tq_target=1024):
    bsz, seq, _ = proj.shape
    blk = WINDOW
    qw = N_KV_HEADS * GQA_GROUP * HEAD_DIM
    kvw = N_KV_HEADS * HEAD_DIM
    tq = _tile(seq, tq_target)
    r = tq // blk
    nb = seq // blk
    kcol = qw // kvw
    vcol = kcol + 1
    in_specs = [
        pl.BlockSpec(memory_space=pltpu.SMEM),
        pl.BlockSpec((1, tq, qw), lambda b, i: (b, i, 0)),
        pl.BlockSpec((1, blk, kvw), lambda b, i: (b, jnp.maximum(i * r - 1, 0), kcol)),
        pl.BlockSpec((1, tq, kvw), lambda b, i: (b, i, kcol)),
        pl.BlockSpec((1, blk, kvw), lambda b, i: (b, jnp.minimum(i * r + r, nb - 1), kcol)),
        pl.BlockSpec((1, blk, kvw), lambda b, i: (b, jnp.maximum(i * r - 1, 0), vcol)),
        pl.BlockSpec((1, tq, kvw), lambda b, i: (b, i, vcol)),
        pl.BlockSpec((1, blk, kvw), lambda b, i: (b, jnp.minimum(i * r + r, nb - 1), vcol)),
    ]
    return pl.pallas_call(
        functools.partial(_wattn_kernel, tq=tq, seq=seq),
        grid=(bsz, seq // tq),
        in_specs=in_specs,
        out_specs=pl.BlockSpec((1, tq, qw), lambda b, i: (b, i, 0)),
        out_shape=jax.ShapeDtypeStruct((bsz, seq, qw), BF16),
        scratch_shapes=[pltpu.VMEM((tq + 2 * blk, kvw), BF16),
                        pltpu.VMEM((tq + 2 * blk, 2 * kvw), BF16)],
        compiler_params=_cparams(("parallel", "parallel")),
    )(sink, proj, proj, proj, proj, proj, proj, proj)


PREP_GROUPS = 8


def _cpow(e, re, im):
    mg = jnp.exp(e * re)
    return mg * jnp.cos(e * im), mg * jnp.sin(e * im)


def _pow_rows(hi, lo):
    (hi_re, hi_im), (lo_re, lo_im) = hi, lo
    rows_re, rows_im = [], []
    for a in range(hi_re.shape[0]):
        ar = hi_re[a:a + 1, :]
        ai = hi_im[a:a + 1, :]
        rows_re.append(ar * lo_re - ai * lo_im)
        rows_im.append(ar * lo_im + ai * lo_re)
    return jnp.concatenate(rows_re, axis=0), jnp.concatenate(rows_im, axis=0)


def _ssm_prep_kernel(rowp_ref, blre_ref, blim_ref, clre_ref, clim_ref, dcol_ref,
                     ktab_ref, win_ref, wout_ref, atab_ref):
    t_len = SSM_CHUNK
    half = SQ // 2
    lane = lax.broadcasted_iota(jnp.int32, (1, SQ), 1)
    is_re_l = (lane % half) < SSM_STATE
    is_f_l = lane < half

    lam_re = rowp_ref[0, 0, 0:1, :]
    lam_im = rowp_ref[0, 0, 1:2, :]
    dt = jnp.exp(rowp_ref[0, 0, 2:3, :])
    re = lam_re * dt
    im = lam_im * dt
    mag = jnp.exp(re)
    n_re = mag * jnp.cos(im) - 1.0
    n_im = mag * jnp.sin(im)
    den = lam_re * lam_re + lam_im * lam_im
    co_re = (n_re * lam_re + n_im * lam_im) / den
    co_im = (n_im * lam_re - n_re * lam_im) / den
    b_re = blre_ref[0, 0]
    b_im = blim_ref[0, 0]
    bb_re = co_re * b_re - co_im * b_im
    bb_im = co_re * b_im + co_im * b_re

    a_i = lax.broadcasted_iota(jnp.int32, (t_len // 8, SQ), 0)
    i_i = lax.broadcasted_iota(jnp.int32, (8, SQ), 0)
    up, dn = 8 * a_i, t_len - 8 - 8 * a_i
    hi_dn_up = _cpow(jnp.where(is_f_l, dn, up).astype(F32), re, im)
    hi_up_dn = _cpow(jnp.where(is_f_l, up, dn).astype(F32), re, im)
    l_re, l_im = _pow_rows(hi_dn_up, _cpow(jnp.where(is_f_l, 7 - i_i, i_i).astype(F32), re, im))
    p1 = jnp.where(is_re_l, bb_re, bb_im)
    p2 = jnp.where(is_re_l, -bb_im, bb_re)
    for c in range(SSM_GROUP):
        win_ref[0, 0, c * t_len:(c + 1) * t_len, :] = (
            l_re * p1[c:c + 1, :] + l_im * p2[c:c + 1, :]).astype(win_ref.dtype)

    mg_t = jnp.exp(t_len * re)
    a_re = mg_t * jnp.cos(t_len * im)
    a_im = mg_t * jnp.sin(t_len * im)
    atab_ref[0, 0, 0:1, :] = a_re
    atab_ref[0, 0, 1:2, :] = jnp.where(is_re_l, -a_im, a_im)
    atab_ref[0, 0, 2:8, :] = jnp.zeros((6, SQ), F32)

    c_re = clre_ref[0, 0]
    c_im = clim_ref[0, 0]
    blocks = []
    for ci in range(SSM_GROUP):
        br = bb_re[ci:ci + 1, :]
        bi = bb_im[ci:ci + 1, :]
        d_re = c_re * br - c_im * bi
        d_im = c_re * bi + c_im * br
        blocks.append(jnp.where(is_re_l, d_re, -d_im))
    dmat = jnp.concatenate(blocks, axis=0)

    t2_re, t2_im = _pow_rows(hi_up_dn, _cpow(jnp.where(is_f_l, i_i, 8 - i_i).astype(F32), re, im))
    k_row = lax.broadcasted_iota(jnp.int32, (t_len, SQ), 0)
    lt = jnp.where(is_re_l, t2_re, t2_im)
    lt = jnp.where(jnp.logical_and(jnp.logical_not(is_f_l), k_row == 0), 0.0, lt)

    def taps_of(lanes):
        return lax.dot_general(dmat[:, lanes], lt[:, lanes], (((1,), (1,)), ((), ())),
                               preferred_element_type=F32, precision=lax.Precision.HIGHEST)

    k_f = taps_of(slice(0, half))
    k_b = taps_of(slice(half, SQ))
    in_b_re = jnp.logical_and(lane >= half, lane < half + SSM_STATE)
    kb0 = jnp.sum(jnp.where(in_b_re, dmat, 0.0), axis=1, keepdims=True)
    tap0 = lax.broadcasted_iota(jnp.int32, (SSM_GROUP * SSM_GROUP, t_len), 1) == 0
    k_f = k_f + jnp.where(tap0, kb0 + dcol_ref[0, 0], 0.0)
    taps = jnp.concatenate([k_f, k_b], axis=1).astype(BF16).astype(F32)
    bits = pltpu.bitcast(taps, jnp.int32)
    ktab_ref[0, 0] = pltpu.roll(bits, 1, axis=1) | lax.shift_right_logical(bits, 16)

    t3_re, t3_im = _pow_rows(hi_up_dn, _cpow(jnp.where(is_f_l, i_i + 1, 8 - i_i).astype(F32), re, im))
    w_cr = jnp.where(is_re_l, t3_re, -t3_im)
    w_ci = jnp.where(is_re_l, -t3_im, -t3_re)
    for c in range(SSM_GROUP):
        wout_ref[0, 0, c * t_len:(c + 1) * t_len, :] = (
            c_re[c:c + 1, :] * w_cr + c_im[c:c + 1, :] * w_ci).astype(wout_ref.dtype)


def _ssm_prep(lam_re, lam_im, log_dt, b_re, b_im, c_re, c_im, d_skip):
    ns, _, n_g, n_p = lam_re.shape
    cg = SSM_GROUP

    def lanes(a):
        return jnp.concatenate([a[:, 0], a[:, 0], a[:, 1], a[:, 1]], axis=-1)

    ldt = jnp.broadcast_to(log_dt[..., None], lam_re.shape)
    rowp = jnp.stack([lanes(lam_re), lanes(lam_im), lanes(ldt)] + [jnp.zeros((ns, n_g, SQ), F32)] * 5, axis=2)
    bl_re = lanes(jnp.swapaxes(b_re, -1, -2))
    bl_im = lanes(jnp.swapaxes(b_im, -1, -2))
    cl_re = lanes(c_re)
    cl_im = lanes(c_im)
    dcol =(jnp.eye(cg, dtype=F32)[None, None] * d_skip.reshape(ns, n_g, 1, cg)).reshape(ns, n_g, cg * cg, 1)

    gb = PREP_GROUPS if n_g % PREP_GROUPS == 0 else 1

    def spec(*shape):
        nd = len(shape)
        return pl.BlockSpec((1, gb) + shape, lambda j, g: (j, g) + (0,) * nd)

    def prep_groups_kernel(*refs):
        for k in range(gb):
            _ssm_prep_kernel(*[r.at[:, k:k + 1] for r in refs])

    return pl.pallas_call(
        prep_groups_kernel,
        grid=(ns, n_g // gb),
        in_specs=[spec(8, SQ), spec(cg, SQ), spec(cg, SQ), spec(cg, SQ), spec(cg, SQ),
                  spec(cg * cg, 1)],
        out_specs=[spec(cg * cg, 2 * SSM_CHUNK), spec(GW, SQ), spec(GW, SQ), spec(8, SQ)],
        out_shape=[jax.ShapeDtypeStruct((ns, n_g, cg * cg, 2 * SSM_CHUNK), jnp.int32),
                   jax.ShapeDtypeStruct((ns, n_g, GW, SQ), BF16),
                   jax.ShapeDtypeStruct((ns, n_g, GW, SQ), BF16),
                   jax.ShapeDtypeStruct((ns, n_g, 8, SQ), F32)],
        compiler_params=_cparams(("parallel", "parallel")),
    )(rowp, bl_re, bl_im, cl_re, cl_im, dcol)


def _gelu_tanh(y):
    return 0.5 * y * (1.0 + jnp.tanh(math.sqrt(2.0 / math.pi) * (y + 0.044715 * (y * y * y))))


SSM_OUT_KSLICE = 256


def _chunk_scan(s_ref, h_ref, a_ref, lo, hi, bsz):
    half = SQ // 2
    cpv = 8 // bsz
    n_tiles = (hi - lo) // 8
    blk = lax.broadcasted_iota(jnp.int32, (8, half), 0) // bsz
    for lanes, order, shift in ((slice(0, half), range(n_tiles), bsz % 8),
                                (slice(half, SQ), range(n_tiles - 1, -1, -1), (8 - bsz) % 8)):
        fwd = lanes.start == 0
        a1 = a_ref[0, 0:1, lanes]
        a2 = a_ref[0, 1:2, lanes]
        cur = jnp.zeros((8, half), F32)
        for k in order:
            rows = slice(lo + 8 * k, lo + 8 * k + 8)
            s_k = s_ref[rows, lanes]
            h_k = jnp.zeros((8, half), F32)
            for j in (range(cpv) if fwd else range(cpv - 1, -1, -1)):
                h_k = jnp.where(blk == j, cur, h_k)
                cur = a1 * cur + a2 * pltpu.roll(cur, SSM_STATE, axis=1) + s_k
                if shift:
                    cur = pltpu.roll(cur, shift, axis=0)
            h_ref[rows, lanes] = h_k


def _ssm_core_kernel(*refs, n_in, bszs):
    x_refs = refs[:n_in]
    ktab_ref, win_ref, wout_ref, a_ref = refs[n_in:n_in + 4]
    z_refs = refs[n_in + 4:2 * n_in + 4]
    m_ref, xb_ref, y_ref, s_ref, h_ref = refs[2 * n_in + 4:]
    t_len = SSM_CHUNK
    offs = [0]
    for x_ref in x_refs:
        offs.append(offs[-1] + x_ref.shape[1])

    for x_ref, lo, hi in zip(x_refs, offs[:-1], offs[1:]):
        xb_ref[lo:hi, :] = x_ref[0].astype(BF16)
    s_ref[...] = jnp.dot(xb_ref[...], win_ref[0], preferred_element_type=F32)
    for lo, hi, bsz in zip(offs[:-1], offs[1:], bszs):
        _chunk_scan(s_ref, h_ref, a_ref, lo, hi, bsz)

    for kk in range(GW // SSM_OUT_KSLICE):
        for ci in range(kk * SSM_OUT_KSLICE // t_len, (kk + 1) * SSM_OUT_KSLICE // t_len):
            for co in range(SSM_GROUP):
                r = ci * SSM_GROUP + co
                p8 = jnp.broadcast_to(ktab_ref[0, r:r + 1, :], (8, 2 * t_len))
                for a in range(t_len // 16):
                    w = pltpu.roll(p8, 16 * a, axis=1, stride=2, stride_axis=0)[:, :t_len]
                    m_ref[ci * t_len + 16 * a:ci * t_len + 16 * a + 16, co * t_len:(co + 1) * t_len] = (
                        pltpu.bitcast(w, m_ref.dtype))
        ks = slice(kk * SSM_OUT_KSLICE, (kk + 1) * SSM_OUT_KSLICE)
        part = jnp.dot(xb_ref[:, ks], m_ref[ks, :], preferred_element_type=F32)
        if kk == 0:
            y_ref[...] = part
        else:
            y_ref[...] += part

    carried = lax.dot_general(h_ref[...].astype(BF16), wout_ref[0], (((1,), (1,)), ((), ())),
                              preferred_element_type=F32)
    for z_ref, lo, hi in zip(z_refs, offs[:-1], offs[1:]):
        z_ref[0] = _gelu_tanh(y_ref[lo:hi, :] + carried[lo:hi, :]).astype(z_ref.dtype)


def _ssm_core(xgs, bszs, ktab, win, wout, atab, layer):
    n_g = ktab.shape[1]
    n_in = len(xgs)
    rows = sum(xg.shape[1] for xg in xgs)
    assert all(8 % b == 0 and xg.shape[1] % 8 == 0 for b, xg in zip(bszs, xgs))

    def table(*shape):
        return pl.BlockSpec((None, 1) + shape, lambda g: (layer, g, 0, 0))

    return pl.pallas_call(
        functools.partial(_ssm_core_kernel, n_in=n_in, bszs=tuple(bszs)),
        grid=(n_g,),
        in_specs=[pl.BlockSpec((1, xg.shape[1], GW), lambda g: (g, 0, 0)) for xg in xgs]
        + [table(SSM_GROUP * SSM_GROUP, 2 * SSM_CHUNK), table(GW, SQ), table(GW, SQ), table(8, SQ)],
        out_specs=[pl.BlockSpec((1, xg.shape[1], GW), lambda g: (g, 0, 0)) for xg in xgs],
        out_shape=[jax.ShapeDtypeStruct((n_g, xg.shape[1], GW), F32) for xg in xgs],
        scratch_shapes=[pltpu.VMEM((GW, GW), BF16),
                        pltpu.VMEM((rows, GW), BF16),
                        pltpu.VMEM((rows, GW), F32),
                        pltpu.VMEM((rows, SQ), F32),
                        pltpu.VMEM((rows, SQ), F32)],
        compiler_params=_cparams(("parallel",)),
    )(*xgs, ktab, win, wout, atab)


def _rope_tables(seq):
    inv_freq = ROPE_THETA ** (-jnp.arange(0, HEAD_DIM, 2, dtype=F32) / HEAD_DIM)
    ang = jnp.arange(seq, dtype=F32)[:, None] * inv_freq[None, :]
    cos, sin = jnp.cos(ang), jnp.sin(ang)
    return jnp.concatenate([cos, cos], axis=-1), jnp.concatenate([-sin, sin], axis=-1)


def kernel(x_prompt, x_sample, mem_prompt, mem_sample, ssm_w_in, ssm_lam_re, ssm_lam_im, ssm_log_dt, ssm_b_re,
           ssm_b_im, ssm_c_re, ssm_c_im, ssm_d, ssm_w_glu, attn_w_in, attn_sink, w_mem_kv, w_out, ln1_g, ln1_b,
           w_ff1, w_ff2, ln2_g, ln2_b):
    depth = w_out.shape[0]
    d_model = x_prompt.shape[-1]
    alpha = (2 * depth) ** 0.25
    mix_w = ssm_d.shape[-1]
    n_groups = mix_w // SSM_GROUP
    mem_w = N_MEM_HEADS * HEAD_DIM

    ssm_w_in_b = ssm_w_in.astype(BF16)
    ssm_w_glu_b = ssm_w_glu.astype(BF16)
    attn_w_in_b = attn_w_in.astype(BF16)
    w_mem_kv_b = w_mem_kv.astype(BF16)
    w_out_b = w_out.astype(BF16)
    w_ff1_b = w_ff1.astype(BF16)
    w_ff2_b = w_ff2.astype(BF16)

    ktab, win, wout, atab = _ssm_prep(ssm_lam_re, ssm_lam_im, ssm_log_dt, ssm_b_re, ssm_b_im,
                                      ssm_c_re, ssm_c_im, ssm_d.reshape(-1, n_groups, SSM_GROUP))
    shapes = [x_prompt.shape, x_sample.shape]
    ropes = [_rope_tables(s[1]) for s in shapes]
    xs = [x_prompt.reshape(-1, d_model), x_sample.reshape(-1, d_model)]
    mems = [mem_prompt, mem_sample]

    ln1_g3, ln1_b3, ln2_g3, ln2_b3 = [a.reshape(depth, 1, d_model) for a in (ln1_g, ln1_b, ln2_g, ln2_b)]

    for i in range(depth):
        j = i // 2
        kvs = [_rowmm(m.reshape(-1, d_model), w_mem_kv_b, i).reshape(m.shape[0], m.shape[1], -1) for m in mems]
        if i % 2 == 0:
            xgs, qms = zip(*[_ssm_inproj(x.reshape(s), ssm_w_in_b, j, mix_w) for x, s in zip(xs, shapes)])
            zs = _ssm_core(xgs, [shp[0] for shp in shapes], ktab, win, wout, atab, j)
            ymix = [_glu(z, ssm_w_glu_b, j, shp[0]) for z, shp in zip(zs, shapes)]
            ymem = [_memattn(q, 0, kv) for q, kv in zip(qms, kvs)]
        else:
            projs = [_attn_inproj(x, attn_w_in_b, j, cos2, sin2, s[1]).reshape(s[0], s[1], -1)
                     for x, s, (cos2, sin2) in zip(xs, shapes, ropes)]
            ymix = [_wattn(p, attn_sink[j]) for p in projs]
            qcol = (mix_w + 2 * N_KV_HEADS * HEAD_DIM) // mem_w
            ymem = [_memattn(p, qcol, kv) for p, kv in zip(projs, kvs)]
        xs = [_oproj(ym.reshape(-1, mix_w), ye.reshape(-1, mem_w), w_out_b, x, ln1_g3, ln1_b3, i, alpha)
              for ym, ye, x in zip(ymix, ymem, xs)]
        xs = [_ffn(x, w_ff1_b, w_ff2_b, ln2_g3, ln2_b3, i, alpha) for x in xs]

    return (xs[0].reshape(shapes[0]), xs[1].reshape(shapes[1]))
```
